```python
import jax, jax.numpy as jnp
from jax import lax
import numpy as np

D_MODEL = 1024
BATCH = 16
SEQ = 4096
DEPTH = 1
DEC_BATCH = 8
DEC_SEQ = 16
PAST_LEN = 2048

CHUNK = 64
HEAD_DIM = 64
A_HEADS = 4
A_HEAD_DIM = 128
A_WIDTH = A_HEADS * A_HEAD_DIM
GMLP_CHUNK = 128
B_HEADS = 8
B_KV_HEADS = 2
GQA_GROUP = B_HEADS // B_KV_HEADS
B_WIDTH = B_HEADS * HEAD_DIM
KV_WIDTH = B_KV_HEADS * HEAD_DIM
WINDOW = 128
WINDOW_CHUNKS = WINDOW // CHUNK
MIX_WIDTH = A_WIDTH + B_WIDTH
IN_WIDTH = 2 * A_WIDTH + B_WIDTH + 2 * KV_WIDTH
D_FF = 2816
CONV_WIDTH = 3
ROPE_THETA = 10000.0
LN_EPS = 1e-5
RMS_EPS = 1e-6
ALPHA = (2 * DEPTH) ** 0.25
BETA = (8 * DEPTH) ** -0.25
ATTN_SCALE = HEAD_DIM ** -0.5

kernel_name = 'hymba_gmlp_swa_convffn_stream_step'


def layer_norm(x, g, b):
    xf = x.astype(jnp.float32)
    mu = jnp.mean(xf, -1, keepdims=True)
    var = jnp.mean(jnp.square(xf - mu), -1, keepdims=True)
    return ((xf - mu) * lax.rsqrt(var + LN_EPS) * g.astype(jnp.float32) + b.astype(jnp.float32)).astype(x.dtype)


def rms_norm(x, g):
    xf = x.astype(jnp.float32)
    ms = jnp.mean(jnp.square(xf), -1, keepdims=True)
    return (xf * lax.rsqrt(ms + RMS_EPS) * g.astype(jnp.float32)).astype(x.dtype)


def rope(x, pos):
    half = HEAD_DIM // 2
    inv = ROPE_THETA ** (-jnp.arange(half, dtype=jnp.float32) / half)
    ang = pos.astype(jnp.float32)[:, None] * inv[None, :]
    cos = jnp.cos(ang)[:, None, :]
    sin = jnp.sin(ang)[:, None, :]
    xf = x.astype(jnp.float32)
    x1, x2 = xf[..., :half], xf[..., half:]
    return jnp.concatenate([x1 * cos - x2 * sin, x2 * cos + x1 * sin], -1).astype(x.dtype)


def gmlp_mask():
    c = jnp.arange(GMLP_CHUNK) // CHUNK
    return c[:, None] >= c[None, :]


def mixer_inputs(x, pos, w_in, ln_g, ln_b):
    bn, t = x.shape[0], x.shape[1]
    z = x @ w_in
    za = jax.nn.gelu(z[..., :2 * A_WIDTH])
    u = za[..., :A_WIDTH].reshape(bn, t, A_HEADS, A_HEAD_DIM)
    gv = layer_norm(za[..., A_WIDTH:], ln_g, ln_b).reshape(bn, t, A_HEADS, A_HEAD_DIM)
    o = 2 * A_WIDTH
    q = rope(z[..., o:o + B_WIDTH].reshape(bn, t, B_HEADS, HEAD_DIM), pos)
    o += B_WIDTH
    k = rope(z[..., o:o + KV_WIDTH].reshape(bn, t, B_KV_HEADS, HEAD_DIM), pos)
    v = z[..., o + KV_WIDTH:].reshape(bn, t, B_KV_HEADS, HEAD_DIM)
    return u, gv, q, k, v


def gmlp_prompt(u, gv, w_s, b_s):
    bn, t = gv.shape[0], gv.shape[1]
    nc = t // GMLP_CHUNK
    vc = gv.reshape(bn, nc, GMLP_CHUNK, A_HEADS, A_HEAD_DIM)
    w = jnp.where(gmlp_mask()[None], w_s, 0)
    s = jnp.einsum('hij,bcjhd->bcihd', w, vc) + b_s.T[None, None, :, :, None]
    return (u * s.reshape(u.shape)).reshape(bn, t, A_WIDTH)


def gmlp_sample(u, gv, w_s, b_s):
    bn, s_len = gv.shape[0], gv.shape[1]
    w = jnp.where(gmlp_mask()[None], w_s, 0)[:, :s_len, :s_len]
    s = jnp.einsum('hij,bjhd->bihd', w, gv) + b_s.T[:s_len][None, :, :, None]
    return (u * s).reshape(bn, s_len, A_WIDTH)


def sink_probs(scores, sinks):
    sk = sinks.astype(jnp.float32).reshape(B_KV_HEADS, GQA_GROUP)[:, :, None, None]
    m = jnp.maximum(jnp.max(scores, -1, keepdims=True), sk)
    e = jnp.exp(scores - m)
    return e / (jnp.sum(e, -1, keepdims=True) + jnp.exp(sk - m))


def swa_prompt(q, k, v, sinks):
    bn, t = q.shape[0], q.shape[1]
    nc = t // CHUNK
    qc = q.reshape(bn, nc, CHUNK, B_KV_HEADS, GQA_GROUP, HEAD_DIM)
    pad = jnp.zeros((bn, WINDOW, B_KV_HEADS, HEAD_DIM), k.dtype)
    kp = jnp.concatenate([pad, k], 1).reshape(bn, nc + WINDOW_CHUNKS, CHUNK, B_KV_HEADS, HEAD_DIM)
    vp = jnp.concatenate([pad, v], 1).reshape(bn, nc + WINDOW_CHUNKS, CHUNK, B_KV_HEADS, HEAD_DIM)
    kb = jnp.concatenate([kp[:, i:i + nc] for i in range(WINDOW_CHUNKS + 1)], axis=2)
    vb = jnp.concatenate([vp[:, i:i + nc] for i in range(WINDOW_CHUNKS + 1)], axis=2)
    key_chunk = (jnp.arange(nc)[:, None] - WINDOW_CHUNKS
                 + jnp.repeat(jnp.arange(WINDOW_CHUNKS + 1), CHUNK)[None, :])
    valid = key_chunk >= 0
    scores = jnp.einsum('bcqkgd,bcskd->bckgqs', qc, kb).astype(jnp.float32) * ATTN_SCALE
    scores = jnp.where(valid[None, :, None, None, None, :], scores, -jnp.inf)
    p = sink_probs(scores, sinks).astype(vb.dtype)
    out = jnp.einsum('bckgqs,bcskd->bcqkgd', p, vb)
    return out.reshape(bn, t, B_WIDTH)


def swa_sample(q, k_all, v_all, sinks):
    bn, s_len = q.shape[0], q.shape[1]
    qg = q.reshape(bn, s_len, B_KV_HEADS, GQA_GROUP, HEAD_DIM)
    scores = jnp.einsum('bqkgd,bskd->bkgqs', qg, k_all).astype(jnp.float32) * ATTN_SCALE
    p = sink_probs(scores, sinks).astype(v_all.dtype)
    out = jnp.einsum('bkgqs,bskd->bqkgd', p, v_all)
    return out.reshape(bn, s_len, B_WIDTH)


def merge_and_ffn(x, ya, yb, conv_prev, norm_a_g, norm_b_g, w_out, ln1_g, ln1_b,
                  w_gate, w_up, conv_w, conv_b, w_down, ln2_g, ln2_b):
    t = x.shape[1]
    m = jnp.concatenate([rms_norm(ya, norm_a_g), rms_norm(yb, norm_b_g)], -1) @ w_out
    x1 = layer_norm(ALPHA * x + m, ln1_g, ln1_b)
    a = x1 @ w_gate
    ap = jnp.concatenate([conv_prev.astype(a.dtype), a], 1)
    c = sum(ap[:, i:i + t] * conv_w[i] for i in range(CONV_WIDTH)) + conv_b
    h = jax.nn.gelu(c) * (x1 @ w_up)
    x2 = layer_norm(ALPHA * x1 + h @ w_down, ln2_g, ln2_b)
    return x2, ap[:, -(CONV_WIDTH - 1):]


def setup_inputs(seed: int = 0) -> dict:
    key = jax.random.key(seed)
    ks = jax.random.split(key, 24)

    def nrm(k, shape, s):
        return jax.random.normal(k, shape, jnp.float32) * s

    return {
        'x_prompt': nrm(ks[0], (BATCH, SEQ, D_MODEL), 1.0),
        'x_sample': nrm(ks[1], (DEC_BATCH, DEC_SEQ, D_MODEL), 1.0),
        'cache_k': nrm(ks[2], (DEPTH, DEC_BATCH, WINDOW, B_KV_HEADS, HEAD_DIM), 1.0),
        'cache_v': nrm(ks[3], (DEPTH, DEC_BATCH, WINDOW, B_KV_HEADS, HEAD_DIM), 1.0),
        'state_ffn_conv': nrm(ks[4], (DEPTH, DEC_BATCH, CONV_WIDTH - 1, D_FF), 1.0),
        'w_in': nrm(ks[5], (DEPTH, D_MODEL, IN_WIDTH), D_MODEL ** -0.5),
        'gmlp_ln_g': 1.0 + nrm(ks[6], (DEPTH, A_WIDTH), 0.02),
        'gmlp_ln_b': nrm(ks[7], (DEPTH, A_WIDTH), 0.02),
        'gmlp_w_s': nrm(ks[8], (DEPTH, A_HEADS, GMLP_CHUNK, GMLP_CHUNK), GMLP_CHUNK ** -0.5),
        'gmlp_b_s': 1.0 + nrm(ks[9], (DEPTH, A_HEADS, GMLP_CHUNK), 0.02),
        'attn_sinks': nrm(ks[10], (DEPTH, B_HEADS), 0.5),
        'norm_a_g': 1.0 + nrm(ks[11], (DEPTH, A_WIDTH), 0.02),
        'norm_b_g': 1.0 + nrm(ks[12], (DEPTH, B_WIDTH), 0.02),
        'w_out': nrm(ks[13], (DEPTH, MIX_WIDTH, D_MODEL), BETA * MIX_WIDTH ** -0.5),
        'ln1_g': 1.0 + nrm(ks[14], (DEPTH, D_MODEL), 0.02),
        'ln1_b': nrm(ks[15], (DEPTH, D_MODEL), 0.02),
        'w_gate': nrm(ks[16], (DEPTH, D_MODEL, D_FF), D_MODEL ** -0.5),
        'w_up': nrm(ks[17], (DEPTH, D_MODEL, D_FF), D_MODEL ** -0.5),
        'conv_w': nrm(ks[18], (DEPTH, CONV_WIDTH, D_FF), CONV_WIDTH ** -0.5),
        'conv_b': nrm(ks[19], (DEPTH, D_FF), 0.02),
        'w_down': nrm(ks[20], (DEPTH, D_FF, D_MODEL), BETA * D_FF ** -0.5),
        'ln2_g': 1.0 + nrm(ks[21], (DEPTH, D_MODEL), 0.02),
        'ln2_b': nrm(ks[22], (DEPTH, D_MODEL), 0.02),
    }


def reference(x_prompt, x_sample, cache_k, cache_v, state_ffn_conv, w_in, gmlp_ln_g, gmlp_ln_b,
              gmlp_w_s, gmlp_b_s, attn_sinks, norm_a_g, norm_b_g, w_out, ln1_g, ln1_b,
              w_gate, w_up, conv_w, conv_b, w_down, ln2_g, ln2_b):
    xp, xs = x_prompt, x_sample
    bp, tp = xp.shape[0], xp.shape[1]
    ts = xs.shape[1]
    pos_p = jnp.arange(tp, dtype=jnp.int32)
    pos_s = PAST_LEN + jnp.arange(ts, dtype=jnp.int32)
    pk, pv, pc, sk, sv, sg, sc = [], [], [], [], [], [], []
    for l in range(DEPTH):
        ffn_args = (norm_a_g[l], norm_b_g[l], w_out[l], ln1_g[l], ln1_b[l],
                    w_gate[l], w_up[l], conv_w[l], conv_b[l], w_down[l], ln2_g[l], ln2_b[l])
        u, gv, q, k, v = mixer_inputs(xp, pos_p, w_in[l], gmlp_ln_g[l], gmlp_ln_b[l])
        ya = gmlp_prompt(u, gv, gmlp_w_s[l], gmlp_b_s[l])
        yb = swa_prompt(q, k, v, attn_sinks[l])
        conv0 = jnp.zeros((bp, CONV_WIDTH - 1, D_FF), xp.dtype)
        xp, conv_p = merge_and_ffn(xp, ya, yb, conv0, *ffn_args)
        pk.append(k[:, -WINDOW:])
        pv.append(v[:, -WINDOW:])
        pc.append(conv_p)
        u, gv, q, k, v = mixer_inputs(xs, pos_s, w_in[l], gmlp_ln_g[l], gmlp_ln_b[l])
        ya = gmlp_sample(u, gv, gmlp_w_s[l], gmlp_b_s[l])
        k_all = jnp.concatenate([cache_k[l].astype(k.dtype), k], 1)
        v_all = jnp.concatenate([cache_v[l].astype(v.dtype), v], 1)
        yb = swa_sample(q, k_all, v_all, attn_sinks[l])
        xs, conv_s = merge_and_ffn(xs, ya, yb, state_ffn_conv[l], *ffn_args)
        sk.append(k)
        sv.append(v)
        sg.append(gv)
        sc.append(conv_s)
    return (xp, xs, jnp.stack(pk), jnp.stack(pv), jnp.stack(pc),
            jnp.stack(sk), jnp.stack(sv), jnp.stack(sg), jnp.stack(sc))
```

```python
import functools
import math

import jax
import jax.numpy as jnp
from jax import lax
from jax.experimental import pallas as pl
from jax.experimental.pallas import tpu as pltpu

D_MODEL = 1024
CHUNK = 64
HEAD_DIM = 64
HALF = HEAD_DIM // 2
A_HEADS = 4
A_HEAD_DIM = 128
A_WIDTH = A_HEADS * A_HEAD_DIM
GMLP_CHUNK = 128
B_HEADS = 8
B_KV_HEADS = 2
GQA_GROUP = B_HEADS // B_KV_HEADS
B_WIDTH = B_HEADS * HEAD_DIM
KV_WIDTH = B_KV_HEADS * HEAD_DIM
WINDOW = 128
MIX_WIDTH = A_WIDTH + B_WIDTH
D_FF = 2816
CONV_WIDTH = 3
PAST_LEN = 2048
ROPE_THETA = 10000.0
LN_EPS = 1e-5
RMS_EPS = 1e-6
DEPTH = 1
ALPHA = (2 * DEPTH) ** 0.25
ATTN_SCALE = HEAD_DIM ** -0.5

LANES = 128
SUBLANES = 8
V7X_VMEM_LIMIT_BYTES = 56 * 1024 * 1024

Q_GROUP_WIDTH = GQA_GROUP * HEAD_DIM
KEY_SPAN = 2 * WINDOW
ROW_BLOCK = 128
FF_CHUNK = 256
N_FF_CHUNKS = D_FF // FF_CHUNK
SEQ_TILE = 256

F32 = jnp.float32
BF16 = jnp.bfloat16


def _gelu(x):
    c0 = math.sqrt(2.0 / math.pi)
    c1 = c0 * 0.044715
    inner = x * (c0 + c1 * (x * x))
    hx = 0.5 * x
    return hx + hx * jnp.tanh(inner)


def _layer_norm(x, g, b):
    mu = jnp.mean(x, -1, keepdims=True)
    xc = x - mu
    var = jnp.mean(xc * xc, -1, keepdims=True)
    return xc * lax.rsqrt(var + LN_EPS) * g + b


def _rms_norm(x, g):
    ms = jnp.mean(x * x, -1, keepdims=True)
    return x * lax.rsqrt(ms + RMS_EPS) * g


def _lane_iota(shape):
    return lax.broadcasted_iota(jnp.int32, shape, 1)


def _row_iota(shape):
    return lax.broadcasted_iota(jnp.int32, shape, 0)


def _rope(x, cos, sin_signed):
    lo = (_lane_iota(cos.shape) % HEAD_DIM) < HALF
    outs = []
    for i in range(x.shape[1] // LANES):
        xs = x[:, i * LANES:(i + 1) * LANES]
        partner = jnp.where(lo, pltpu.roll(xs, LANES - HALF, 1), pltpu.roll(xs, HALF, 1))
        outs.append(xs * cos + partner * sin_signed)
    return outs[0] if len(outs) == 1 else jnp.concatenate(outs, 1)


def _replicate_kv_heads(x):
    lo = _lane_iota(x.shape) < HEAD_DIM
    sw = pltpu.roll(x, HEAD_DIM, 1)
    return jnp.where(lo, x, sw), jnp.where(lo, sw, x)


def _head_lane_masks(rows, value):
    lane = _lane_iota((rows, Q_GROUP_WIDTH))
    return [jnp.where((lane >= h * HEAD_DIM) & (lane < (h + 1) * HEAD_DIM), value, 0.0).astype(BF16)
            for h in range(GQA_GROUP)]


def _attn_block(qg, krep, vrep, allowed, sinks, qmasks, vmasks):
    r = qg.shape[0]
    k2 = jnp.concatenate([krep, krep], 1)
    v2 = jnp.concatenate([vrep, vrep], 1)
    qm = jnp.concatenate([qg * qmasks[h] for h in range(GQA_GROUP)], 0)
    sc = lax.dot_general(qm, k2, (((1,), (1,)), ((), ())), preferred_element_type=F32)
    ps = []
    for h in range(GQA_GROUP):
        s = jnp.where(allowed, sc[h * r:(h + 1) * r], -jnp.inf)
        m = jnp.maximum(jnp.max(s, -1, keepdims=True), sinks[h])
        e = jnp.exp(s - m)
        den = jnp.sum(e, -1, keepdims=True) + jnp.exp(sinks[h] - m)
        ps.append((e * (1.0 / den)).astype(BF16))
    pc = jnp.concatenate(ps, 1)
    vm = jnp.concatenate([v2 * vmasks[h] for h in range(GQA_GROUP)], 0)
    return jnp.dot(pc, vm, preferred_element_type=F32)


def _mixer_inputs(xb, win_ref, lng, lnb, cos, sin_signed):
    za = jnp.dot(xb, win_ref[:, :2 * A_WIDTH], preferred_element_type=F32)
    u = _gelu(za[:, :A_WIDTH])
    gv = _layer_norm(_gelu(za[:, A_WIDTH:]), lng, lnb)
    zb = jnp.dot(xb, win_ref[:, 2 * A_WIDTH:], preferred_element_type=F32)
    q = _rope(zb[:, :B_WIDTH], cos, sin_signed)
    k = _rope(zb[:, B_WIDTH:B_WIDTH + KV_WIDTH], cos, sin_signed)
    v = zb[:, B_WIDTH + KV_WIDTH:]
    return u, gv, q, k, v


def _merge(x, mix, nag, nbg, wout_ref, ln1g, ln1b):
    mi = jnp.concatenate([_rms_norm(mix[:, :A_WIDTH], nag), _rms_norm(mix[:, A_WIDTH:], nbg)], 1)
    m = jnp.dot(mi.astype(BF16), wout_ref[...], preferred_element_type=F32)
    return _layer_norm(ALPHA * x + m, ln1g, ln1b)


def _prompt_kernel(sinks_ref, x_ref, cos_ref, sin_ref, win_ref, lng_ref, lnb_ref, ws_ref, bs_ref,
                   nag_ref, nbg_ref, wout_ref, ln1g_ref, ln1b_ref, wg_ref, wu_ref, cw_ref, cb_ref,
                   wd_ref, ln2g_ref, ln2b_ref,
                   y_ref, pk_ref, pv_ref, pc_ref,
                   kv_scr, conv_scr, mix_scr, h_scr, *, tile):
    t = pl.program_id(1)

    @pl.when(t == 0)
    def _():
        kv_scr[...] = jnp.zeros_like(kv_scr)
        conv_scr[...] = jnp.zeros_like(conv_scr)

    x = x_ref[0]
    xb = x.astype(BF16)
    row0 = pl.multiple_of(t * tile, tile)
    cos = cos_ref[pl.ds(row0, tile), :]
    sin_signed = sin_ref[pl.ds(row0, tile), :]
    u, gv, q, k, v = _mixer_inputs(xb, win_ref, lng_ref[...], lnb_ref[...], cos, sin_signed)
    pk_ref[0] = k[tile - WINDOW:]
    pv_ref[0] = v[tile - WINDOW:]

    cidx_r = _row_iota((GMLP_CHUNK, GMLP_CHUNK)) // CHUNK
    cidx_c = _lane_iota((GMLP_CHUNK, GMLP_CHUNK)) // CHUNK
    gvb = gv.astype(BF16)
    for h in range(A_HEADS):
        wm = jnp.where(cidx_r >= cidx_c, ws_ref[h], 0.0).astype(BF16)
        cols = slice(h * A_HEAD_DIM, (h + 1) * A_HEAD_DIM)
        for c in range(tile // GMLP_CHUNK):
            rows = slice(c * GMLP_CHUNK, (c + 1) * GMLP_CHUNK)
            s = jnp.dot(wm, gvb[rows, cols], preferred_element_type=F32) + bs_ref[h]
            mix_scr[rows, cols] = u[rows, cols] * s

    qb = q.astype(BF16)
    k0, k1 = _replicate_kv_heads(k)
    v0, v1 = _replicate_kv_heads(v)
    reps = [jnp.concatenate([kv_scr[i], a.astype(BF16)], 0) for i, a in enumerate((k0, k1, v0, v1))]
    for i in range(4):
        kv_scr[i] = reps[i][tile:]
    qmasks = _head_lane_masks(ROW_BLOCK, ATTN_SCALE)
    vmasks = _head_lane_masks(KEY_SPAN, 1.0)
    qc = _row_iota((ROW_BLOCK, KEY_SPAN)) // CHUNK
    kc = _lane_iota((ROW_BLOCK, KEY_SPAN)) // CHUNK
    in_band = (kc >= qc) & (kc <= qc + WINDOW // CHUNK)
    first_lo = jnp.where(t > 0, 0, WINDOW // CHUNK)
    allowed_first = in_band & (kc >= first_lo)
    for j in range(tile // ROW_BLOCK):
        rows = slice(j * ROW_BLOCK, (j + 1) * ROW_BLOCK)
        krows = slice(j * ROW_BLOCK, j * ROW_BLOCK + KEY_SPAN)
        allowed = allowed_first if j == 0 else in_band
        for g in range(B_KV_HEADS):
            sinks = [sinks_ref[g * GQA_GROUP + h] for h in range(GQA_GROUP)]
            qcols = slice(g * Q_GROUP_WIDTH, (g + 1) * Q_GROUP_WIDTH)
            out = _attn_block(qb[rows, qcols], reps[g][krows], reps[2 + g][krows], allowed, sinks,
                              qmasks, vmasks)
            mix_scr[rows, A_WIDTH + g * Q_GROUP_WIDTH:A_WIDTH + (g + 1) * Q_GROUP_WIDTH] = out

    x1 = _merge(x, mix_scr[...], nag_ref[...], nbg_ref[...], wout_ref, ln1g_ref[...], ln1b_ref[...])
    x1b = x1.astype(BF16)

    for c in range(N_FF_CHUNKS):
        cs = slice(c * FF_CHUNK, (c + 1) * FF_CHUNK)
        a = jnp.dot(x1b, wg_ref[:, cs], preferred_element_type=F32)
        up = jnp.dot(x1b, wu_ref[:, cs], preferred_element_type=F32)
        a_ext = jnp.concatenate([conv_scr[:, cs], a], 0)
        a1 = a_ext[SUBLANES - 1:SUBLANES - 1 + tile]
        a2 = a_ext[SUBLANES - 2:SUBLANES - 2 + tile]
        cc = a2 * cw_ref[0:1, cs] + a1 * cw_ref[1:2, cs] + a * cw_ref[2:3, cs] + cb_ref[:, cs]
        h_scr[:, cs] = (_gelu(cc) * up).astype(BF16)
        conv_scr[:, cs] = a[tile - SUBLANES:]
        pc_ref[0, :, cs] = a[tile - SUBLANES:]
    f = jnp.dot(h_scr[...], wd_ref[...], preferred_element_type=F32)
    y_ref[0] = _layer_norm(ALPHA * x1 + f, ln2g_ref[...], ln2b_ref[...])


def _sample_kernel(sinks_ref, x_ref, cos_ref, sin_ref, ck_ref, cv_ref, s1_ref, s2_ref,
                   win_ref, lng_ref, lnb_ref, wt_ref, bs_ref,
                   nag_ref, nbg_ref, wout_ref, ln1g_ref, ln1b_ref, wg_ref, wu_ref, cw_ref, cb_ref,
                   wd_ref, ln2g_ref, ln2b_ref,
                   y_ref, k_ref, v_ref, gv_ref, a_ref,
                   mix_scr, h_scr, *, n_batch, s_len):
    rows_all = n_batch * s_len
    x = x_ref[...]
    xb = x.astype(BF16)
    u, gv, q, k, v = _mixer_inputs(xb, win_ref, lng_ref[...], lnb_ref[...], cos_ref[...], sin_ref[...])
    gv_ref[...] = gv
    k_ref[...] = k
    v_ref[...] = v

    ri = _row_iota((rows_all, rows_all))
    ci = _lane_iota((rows_all, rows_all))
    same_batch = (ri // s_len) == (ci // s_len)
    causal = ((ri % s_len) // CHUNK) >= ((ci % s_len) // CHUNK)
    gvb = gv.astype(BF16)
    for h in range(A_HEADS):
        wm = jnp.where(same_batch & causal, wt_ref[h], 0.0).astype(BF16)
        cols = slice(h * A_HEAD_DIM, (h + 1) * A_HEAD_DIM)
        s = jnp.dot(wm, gvb[:, cols], preferred_element_type=F32) + bs_ref[h]
        mix_scr[:, cols] = u[:, cols] * s

    qb = q.astype(BF16)
    qmasks = _head_lane_masks(s_len, ATTN_SCALE)
    vmasks = _head_lane_masks(KEY_SPAN, 1.0)
    allowed = _lane_iota((s_len, KEY_SPAN)) < WINDOW + s_len
    pad = jnp.zeros((KEY_SPAN - WINDOW - s_len, LANES), F32)
    for b in range(n_batch):
        rows = slice(b * s_len, (b + 1) * s_len)
        k_all = jnp.concatenate([ck_ref[b], k[rows], pad], 0)
        v_all = jnp.concatenate([cv_ref[b], v[rows], pad], 0)
        kreps = [a.astype(BF16) for a in _replicate_kv_heads(k_all)]
        vreps = [a.astype(BF16) for a in _replicate_kv_heads(v_all)]
        for g in range(B_KV_HEADS):
            sinks = [sinks_ref[g * GQA_GROUP + h] for h in range(GQA_GROUP)]
            qcols = slice(g * Q_GROUP_WIDTH, (g + 1) * Q_GROUP_WIDTH)
            out = _attn_block(qb[rows, qcols], kreps[g], vreps[g], allowed, sinks, qmasks, vmasks)
            mix_scr[rows, A_WIDTH + g * Q_GROUP_WIDTH:A_WIDTH + (g + 1) * Q_GROUP_WIDTH] = out

    x1 = _merge(x, mix_scr[...], nag_ref[...], nbg_ref[...], wout_ref, ln1g_ref[...], ln1b_ref[...])
    x1b = x1.astype(BF16)

    pos = _row_iota((rows_all, FF_CHUNK)) % s_len
    for c in range(N_FF_CHUNKS):
        cs = slice(c * FF_CHUNK, (c + 1) * FF_CHUNK)
        a = jnp.dot(x1b, wg_ref[:, cs], preferred_element_type=F32)
        up = jnp.dot(x1b, wu_ref[:, cs], preferred_element_type=F32)
        a_ref[:, cs] = a
        a1 = jnp.where(pos < 1, s1_ref[:, cs], pltpu.roll(a, 1, 0))
        a2 = jnp.where(pos < 2, s2_ref[:, cs], pltpu.roll(a, 2, 0))
        cc = a2 * cw_ref[0:1, cs] + a1 * cw_ref[1:2, cs] + a * cw_ref[2:3, cs] + cb_ref[:, cs]
        h_scr[:, cs] = (_gelu(cc) * up).astype(BF16)
    f = jnp.dot(h_scr[...], wd_ref[...], preferred_element_type=F32)
    y_ref[...] = _layer_norm(ALPHA * x1 + f, ln2g_ref[...], ln2b_ref[...])


def _rope_tables(pos):
    inv = ROPE_THETA ** (-jnp.arange(HALF, dtype=F32) / HALF)
    ang = pos.astype(F32)[:, None] * inv[None, :]
    cos, sin = jnp.cos(ang), jnp.sin(ang)
    reps = LANES // HEAD_DIM
    return (jnp.tile(jnp.concatenate([cos, cos], -1), (1, reps)),
            jnp.tile(jnp.concatenate([-sin, sin], -1), (1, reps)))


def _resident(shape):
    return pl.BlockSpec(shape, lambda *_: (0,) * len(shape), pipeline_mode=pl.Buffered(1))


def kernel(x_prompt, x_sample, cache_k, cache_v, state_ffn_conv, w_in, gmlp_ln_g, gmlp_ln_b,
           gmlp_w_s, gmlp_b_s, attn_sinks, norm_a_g, norm_b_g, w_out, ln1_g, ln1_b,
           w_gate, w_up, conv_w, conv_b, w_down, ln2_g, ln2_b):
    assert w_in.shape[0] == DEPTH == 1
    bp, tp, _ = x_prompt.shape
    bs, ts, _ = x_sample.shape
    tile = SEQ_TILE
    assert tp % tile == 0 and tile % ROW_BLOCK == 0 and tile >= WINDOW
    assert bs * ts == ROW_BLOCK and WINDOW + ts <= KEY_SPAN and ts >= CONV_WIDTH - 1

    row = lambda a: a[0].reshape(1, -1)
    win_b, wout_b = w_in[0].astype(BF16), w_out[0].astype(BF16)
    wg_b, wu_b, wd_b = w_gate[0].astype(BF16), w_up[0].astype(BF16), w_down[0].astype(BF16)
    sinks = attn_sinks[0]
    vec_args = dict(lng=row(gmlp_ln_g), lnb=row(gmlp_ln_b), nag=row(norm_a_g), nbg=row(norm_b_g),
                    ln1g=row(ln1_g), ln1b=row(ln1_b), cb=row(conv_b), ln2g=row(ln2_g), ln2b=row(ln2_b))
    cw = conv_w[0]
    smem = pl.BlockSpec(memory_space=pltpu.SMEM)

    def weight_specs(ws_shape):
        return [_resident(win_b.shape), _resident((1, A_WIDTH)), _resident((1, A_WIDTH)),
                _resident(ws_shape), _resident(ws_shape),
                _resident((1, A_WIDTH)), _resident((1, B_WIDTH)), _resident(wout_b.shape),
                _resident((1, D_MODEL)), _resident((1, D_MODEL)),
                _resident(wg_b.shape), _resident(wu_b.shape), _resident(cw.shape), _resident((1, D_FF)),
                _resident(wd_b.shape), _resident((1, D_MODEL)), _resident((1, D_MODEL))]

    def weight_args(ws, bsb):
        return (win_b, vec_args['lng'], vec_args['lnb'], ws, bsb, vec_args['nag'], vec_args['nbg'],
                wout_b, vec_args['ln1g'], vec_args['ln1b'], wg_b, wu_b, cw, vec_args['cb'], wd_b,
                vec_args['ln2g'], vec_args['ln2b'])

    cos_p, sin_p = _rope_tables(jnp.arange(tp, dtype=jnp.int32))
    bias_p = jnp.broadcast_to(gmlp_b_s[0][:, :, None], (A_HEADS, GMLP_CHUNK, A_HEAD_DIM))
    y_p, pk, pv, pc = pl.pallas_call(
        functools.partial(_prompt_kernel, tile=tile),
        grid=(bp, tp // tile),
        in_specs=[smem,
                  pl.BlockSpec((1, tile, D_MODEL), lambda b, t: (b, t, 0)),
                  _resident(cos_p.shape), _resident(sin_p.shape)]
                 + weight_specs((A_HEADS, GMLP_CHUNK, GMLP_CHUNK)),
        out_specs=[pl.BlockSpec((1, tile, D_MODEL), lambda b, t: (b, t, 0)),
                   pl.BlockSpec((1, WINDOW, KV_WIDTH), lambda b, t: (b, 0, 0)),
                   pl.BlockSpec((1, WINDOW, KV_WIDTH), lambda b, t: (b, 0, 0)),
                   pl.BlockSpec((1, SUBLANES, D_FF), lambda b, t: (b, 0, 0))],
        out_shape=[jax.ShapeDtypeStruct((bp, tp, D_MODEL), F32),
                   jax.ShapeDtypeStruct((bp, WINDOW, KV_WIDTH), F32),
                   jax.ShapeDtypeStruct((bp, WINDOW, KV_WIDTH), F32),
                   jax.ShapeDtypeStruct((bp, SUBLANES, D_FF), F32)],
        scratch_shapes=[pltpu.VMEM((4, WINDOW, LANES), BF16),
                        pltpu.VMEM((SUBLANES, D_FF), F32),
                        pltpu.VMEM((tile, MIX_WIDTH), F32),
                        pltpu.VMEM((tile, D_FF), BF16)],
        compiler_params=pltpu.CompilerParams(dimension_semantics=("arbitrary", "arbitrary"),
                                             vmem_limit_bytes=V7X_VMEM_LIMIT_BYTES),
        name="prompt_layer",
    )(sinks, x_prompt, cos_p, sin_p, *weight_args(gmlp_w_s[0], bias_p))

    rows_all = bs * ts
    cos_s, sin_s = _rope_tables(PAST_LEN + jnp.arange(ts, dtype=jnp.int32))
    cos_s, sin_s = jnp.tile(cos_s, (bs, 1)), jnp.tile(sin_s, (bs, 1))
    ck = cache_k[0].reshape(bs, WINDOW, KV_WIDTH)
    cv = cache_v[0].reshape(bs, WINDOW, KV_WIDTH)
    st = state_ffn_conv[0]
    zeros = jnp.zeros((bs, ts, D_FF), F32)
    s1 = zeros.at[:, 0].set(st[:, 1]).reshape(rows_all, D_FF)
    s2 = zeros.at[:, 0].set(st[:, 0]).at[:, 1].set(st[:, 1]).reshape(rows_all, D_FF)
    w_tiled = jnp.tile(gmlp_w_s[0][:, :ts, :ts], (1, bs, bs))
    bias_s = jnp.broadcast_to(jnp.tile(gmlp_b_s[0][:, :ts], (1, bs))[:, :, None],
                              (A_HEADS, rows_all, A_HEAD_DIM))
    full = lambda shape: pl.BlockSpec(shape, lambda i: (0,) * len(shape))
    y_s, k_s, v_s, gv_s, a_s = pl.pallas_call(
        functools.partial(_sample_kernel, n_batch=bs, s_len=ts),
        grid=(1,),
        in_specs=[smem, _resident((rows_all, D_MODEL)), _resident(cos_s.shape), _resident(sin_s.shape),
                  _resident(ck.shape), _resident(cv.shape), _resident(s1.shape), _resident(s2.shape)]
                 + weight_specs((A_HEADS, rows_all, rows_all)),
        out_specs=[full((rows_all, D_MODEL)), full((rows_all, KV_WIDTH)), full((rows_all, KV_WIDTH)),
                   full((rows_all, A_WIDTH)), full((rows_all, D_FF))],
        out_shape=[jax.ShapeDtypeStruct((rows_all, D_MODEL), F32),
                   jax.ShapeDtypeStruct((rows_all, KV_WIDTH), F32),
                   jax.ShapeDtypeStruct((rows_all, KV_WIDTH), F32),
                   jax.ShapeDtypeStruct((rows_all, A_WIDTH), F32),
                   jax.ShapeDtypeStruct((rows_all, D_FF), F32)],
        scratch_shapes=[pltpu.VMEM((rows_all, MIX_WIDTH), F32),
                        pltpu.VMEM((rows_all, D_FF), BF16)],
        compiler_params=pltpu.CompilerParams(dimension_semantics=("arbitrary",),
                                             vmem_limit_bytes=V7X_VMEM_LIMIT_BYTES),
        name="sample_layer",
    )(sinks, x_sample.reshape(rows_all, D_MODEL), cos_s, sin_s, ck, cv, s1, s2,
      *weight_args(w_tiled, bias_s))

    kv5 = lambda a, n, t: a.reshape(1, n, t, B_KV_HEADS, HEAD_DIM)
    return (y_p, y_s.reshape(bs, ts, D_MODEL),
            kv5(pk, bp, WINDOW), kv5(pv, bp, WINDOW),
            pc[None, :, SUBLANES - (CONV_WIDTH - 1):],
            kv5(k_s, bs, ts), kv5(v_s, bs, ts),
            gv_s.reshape(1, bs, ts, A_HEADS, A_HEAD_DIM),
            a_s.reshape(bs, ts, D_FF)[None, :, ts - (CONV_WIDTH - 1):])
```

```python
import functools
import math

import jax
import jax.numpy as jnp
from jax import lax
from jax.experimental import pallas as pl
from jax.experimental.pallas import tpu as pltpu

D_MODEL = 1024
CHUNK = 64
HEAD_DIM = 64
HALF = HEAD_DIM // 2
A_HEADS = 4
A_HEAD_DIM = 128
A_WIDTH = A_HEADS * A_HEAD_DIM
GMLP_CHUNK = 128
B_HEADS = 8
B_KV_HEADS = 2
GQA_GROUP = B_HEADS // B_KV_HEADS
B_WIDTH = B_HEADS * HEAD_DIM
KV_WIDTH = B_KV_HEADS * HEAD_DIM
WINDOW = 128
MIX_WIDTH = A_WIDTH + B_WIDTH
D_FF = 2816
CONV_WIDTH = 3
PAST_LEN = 2048
ROPE_THETA = 10000.0
LN_EPS = 1e-5
RMS_EPS = 1e-6
DEPTH = 1
ALPHA = (2 * DEPTH) ** 0.25
ATTN_SCALE = HEAD_DIM ** -0.5

LANES = 128
SUBLANES = 8
V7X_VMEM_LIMIT_BYTES = 56 * 1024 * 1024

Q_GROUP_WIDTH = GQA_GROUP * HEAD_DIM
KEY_SPAN = 2 * WINDOW
ROW_BLOCK = 128
FF_CHUNK = 256
N_FF_CHUNKS = D_FF // FF_CHUNK
DOWN_BLOCK = 256
SEQ_TILE = 256

F32 = jnp.float32
BF16 = jnp.bfloat16


def _gelu(x):
    c0 = math.sqrt(2.0 / math.pi)
    c1 = c0 * 0.044715
    inner = x * (c0 + c1 * (x * x))
    hx = 0.5 * x
    return hx + hx * jnp.tanh(inner)


def _layer_norm(x, g, b):
    mu = jnp.mean(x, -1, keepdims=True)
    xc = x - mu
    var = jnp.mean(xc * xc, -1, keepdims=True)
    return xc * lax.rsqrt(var + LN_EPS) * g + b


def _rms_norm(x, g):
    ms = jnp.mean(x * x, -1, keepdims=True)
    return x * lax.rsqrt(ms + RMS_EPS) * g


def _lane_iota(shape):
    return lax.broadcasted_iota(jnp.int32, shape, 1)


def _row_iota(shape):
    return lax.broadcasted_iota(jnp.int32, shape, 0)


def _rope(x, cos, sin_signed):
    lo = (_lane_iota(cos.shape) % HEAD_DIM) < HALF
    outs = []
    for i in range(x.shape[1] // LANES):
        xs = x[:, i * LANES:(i + 1) * LANES]
        partner = jnp.where(lo, pltpu.roll(xs, LANES - HALF, 1), pltpu.roll(xs, HALF, 1))
        outs.append(xs * cos + partner * sin_signed)
    return outs[0] if len(outs) == 1 else jnp.concatenate(outs, 1)


def _replicate_kv_heads(x):
    lo = _lane_iota(x.shape) < HEAD_DIM
    sw = pltpu.roll(x, HEAD_DIM, 1)
    return jnp.where(lo, x, sw), jnp.where(lo, sw, x)


def _head_lane_masks(rows, value):
    lane = _lane_iota((rows, Q_GROUP_WIDTH))
    return [jnp.where((lane >= h * HEAD_DIM) & (lane < (h + 1) * HEAD_DIM), value, 0.0).astype(BF16)
            for h in range(GQA_GROUP)]


def _attn_scores(qg, krep, qmasks):
    k2 = jnp.concatenate([krep, krep], 1)
    qm = jnp.concatenate([qg * qmasks[h] for h in range(GQA_GROUP)], 0)
    return lax.dot_general(qm, k2, (((1,), (1,)), ((), ())), preferred_element_type=F32)


def _attn_probs(sc, allowed, sinks):
    r = sc.shape[0] // GQA_GROUP
    ps = []
    for h in range(GQA_GROUP):
        s = jnp.where(allowed, sc[h * r:(h + 1) * r], -jnp.inf)
        m = jnp.maximum(jnp.max(s, -1, keepdims=True), sinks[h])
        e = jnp.exp(s - m)
        den = jnp.sum(e, -1, keepdims=True) + jnp.exp(sinks[h] - m)
        ps.append((e * (1.0 / den)).astype(BF16))
    return jnp.concatenate(ps, 1)


def _attn_values(pc, vrep, vmasks):
    v2 = jnp.concatenate([vrep, vrep], 1)
    vm = jnp.concatenate([v2 * vmasks[h] for h in range(GQA_GROUP)], 0)
    return jnp.dot(pc, vm, preferred_element_type=F32)


def _attn_block(qg, krep, vrep, allowed, sinks, qmasks, vmasks):
    return _attn_values(_attn_probs(_attn_scores(qg, krep, qmasks), allowed, sinks), vrep, vmasks)


def _mixer_inputs(xb, win_ref, lng, lnb, cos, sin_signed):
    za = jnp.dot(xb, win_ref[:, :2 * A_WIDTH], preferred_element_type=F32)
    u = _gelu(za[:, :A_WIDTH])
    gv = _layer_norm(_gelu(za[:, A_WIDTH:]), lng, lnb)
    zb = jnp.dot(xb, win_ref[:, 2 * A_WIDTH:], preferred_element_type=F32)
    q = _rope(zb[:, :B_WIDTH], cos, sin_signed)
    k = _rope(zb[:, B_WIDTH:B_WIDTH + KV_WIDTH], cos, sin_signed)
    v = zb[:, B_WIDTH + KV_WIDTH:]
    return u, gv, q, k, v


def _merge(x, mix, nag, nbg, wout_ref, ln1g, ln1b):
    mi = jnp.concatenate([_rms_norm(mix[:, :A_WIDTH], nag), _rms_norm(mix[:, A_WIDTH:], nbg)], 1)
    m = jnp.dot(mi.astype(BF16), wout_ref[...], preferred_element_type=F32)
    return _layer_norm(ALPHA * x + m, ln1g, ln1b)


def _conv_taps(a, prev8):
    r1 = pltpu.roll(a, 1, 0)
    r2 = pltpu.roll(a, 2, 0)
    row = _row_iota(prev8.shape)
    first1 = jnp.where(row < 1, pltpu.roll(prev8, 1, 0), r1[:SUBLANES])
    first2 = jnp.where(row < 2, pltpu.roll(prev8, 2, 0), r2[:SUBLANES])
    return (jnp.concatenate([first1, r1[SUBLANES:]], 0), jnp.concatenate([first2, r2[SUBLANES:]], 0))


def _prompt_kernel(sinks_ref, x_ref, cos_ref, sin_ref, win_ref, lng_ref, lnb_ref, ws_ref, bs_ref,
                   nag_ref, nbg_ref, wout_ref, ln1g_ref, ln1b_ref, wg_ref, wu_ref, cw_ref, cb_ref,
                   wd_ref, ln2g_ref, ln2b_ref,
                   y_ref, pk_ref, pv_ref, pc_ref,
                   kv_scr, conv_scr, mix_scr, h_scr, pre1_scr, pre2_scr, x1c_scr, x1b_scr,
                   *, tile, seq_tiles, n_tiles):
    s = pl.program_id(0)
    tf = jnp.minimum(s, n_tiles - 1) % seq_tiles
    tb = jnp.clip(s - 1, 0, n_tiles - 1) % seq_tiles

    @pl.when(s == 0)
    def _():
        pre1_scr[...] = jnp.zeros_like(pre1_scr)
        pre2_scr[...] = jnp.zeros_like(pre2_scr)

    @pl.when(tf == 0)
    def _():
        kv_scr[...] = jnp.zeros_like(kv_scr)

    @pl.when(tb == 0)
    def _():
        conv_scr[...] = jnp.zeros_like(conv_scr)

    st = {}

    def back_ln1():
        x1 = _layer_norm(pre1_scr[...], ln1g_ref[...], ln1b_ref[...])
        x1c_scr[...] = x1
        x1b_scr[...] = x1.astype(BF16)

    def out_ln2():
        y_ref[0] = _layer_norm(pre2_scr[...], ln2g_ref[...], ln2b_ref[...])

    def front_dot_a():
        st['xb'] = x_ref[0].astype(BF16)
        st['za'] = jnp.dot(st['xb'], win_ref[:, :2 * A_WIDTH], preferred_element_type=F32)

    def front_dot_b():
        st['zb'] = jnp.dot(st['xb'], win_ref[:, 2 * A_WIDTH:], preferred_element_type=F32)

    def front_epi_a():
        za = st.pop('za')
        st['u'] = _gelu(za[:, :A_WIDTH])
        st['gv'] = _layer_norm(_gelu(za[:, A_WIDTH:]), lng_ref[...], lnb_ref[...])

    def front_epi_b():
        zb = st.pop('zb')
        row0 = pl.multiple_of(tf * tile, tile)
        cos = cos_ref[pl.ds(row0, tile), :]
        sin_signed = sin_ref[pl.ds(row0, tile), :]
        k = _rope(zb[:, B_WIDTH:B_WIDTH + KV_WIDTH], cos, sin_signed)
        v = zb[:, B_WIDTH + KV_WIDTH:]
        pk_ref[0] = k[tile - WINDOW:]
        pv_ref[0] = v[tile - WINDOW:]
        st.update(q=_rope(zb[:, :B_WIDTH], cos, sin_signed), k=k, v=v)

    def front_gmlp():
        cidx_r = _row_iota((GMLP_CHUNK, GMLP_CHUNK)) // CHUNK
        cidx_c = _lane_iota((GMLP_CHUNK, GMLP_CHUNK)) // CHUNK
        gvb = st['gv'].astype(BF16)
        for h in range(A_HEADS):
            wm = jnp.where(cidx_r >= cidx_c, ws_ref[h], 0.0).astype(BF16)
            cols = slice(h * A_HEAD_DIM, (h + 1) * A_HEAD_DIM)
            for c in range(tile // GMLP_CHUNK):
                rows = slice(c * GMLP_CHUNK, (c + 1) * GMLP_CHUNK)
                sg = jnp.dot(wm, gvb[rows, cols], preferred_element_type=F32) + bs_ref[h]
                mix_scr[rows, cols] = st['u'][rows, cols] * sg

    def front_attn_prep():
        k0, k1 = _replicate_kv_heads(st['k'])
        v0, v1 = _replicate_kv_heads(st['v'])
        reps = [jnp.concatenate([kv_scr[i], a.astype(BF16)], 0) for i, a in enumerate((k0, k1, v0, v1))]
        for i in range(4):
            kv_scr[i] = reps[i][tile:]
        qc = _row_iota((ROW_BLOCK, KEY_SPAN)) // CHUNK
        kc = _lane_iota((ROW_BLOCK, KEY_SPAN)) // CHUNK
        in_band = (kc >= qc) & (kc <= qc + WINDOW // CHUNK)
        first_lo = jnp.where(tf > 0, 0, WINDOW // CHUNK)
        st.update(reps=reps, qb=st['q'].astype(BF16), in_band=in_band,
                  allowed_first=in_band & (kc >= first_lo),
                  qmasks=_head_lane_masks(ROW_BLOCK, ATTN_SCALE), vmasks=_head_lane_masks(KEY_SPAN, 1.0))

    attn_blocks = [(j, g) for j in range(tile // ROW_BLOCK) for g in range(B_KV_HEADS)]

    def attn_qk(i):
        j, g = attn_blocks[i]
        qg = st['qb'][j * ROW_BLOCK:(j + 1) * ROW_BLOCK, g * Q_GROUP_WIDTH:(g + 1) * Q_GROUP_WIDTH]
        st['sc', i] = _attn_scores(qg, st['reps'][g][j * ROW_BLOCK:j * ROW_BLOCK + KEY_SPAN], st['qmasks'])

    def attn_softmax(i):
        j, g = attn_blocks[i]
        sinks = [sinks_ref[g * GQA_GROUP + h] for h in range(GQA_GROUP)]
        allowed = st['allowed_first'] if j == 0 else st['in_band']
        st['p', i] = _attn_probs(st.pop(('sc', i)), allowed, sinks)

    def attn_pv(i):
        j, g = attn_blocks[i]
        out = _attn_values(st.pop(('p', i)), st['reps'][2 + g][j * ROW_BLOCK:j * ROW_BLOCK + KEY_SPAN],
                           st['vmasks'])
        mix_scr[j * ROW_BLOCK:(j + 1) * ROW_BLOCK,
                A_WIDTH + g * Q_GROUP_WIDTH:A_WIDTH + (g + 1) * Q_GROUP_WIDTH] = out

    def front_rms():
        mix = mix_scr[...]
        st['mi'] = jnp.concatenate([_rms_norm(mix[:, :A_WIDTH], nag_ref[...]),
                                    _rms_norm(mix[:, A_WIDTH:], nbg_ref[...])], 1).astype(BF16)

    def front_wout():
        m = jnp.dot(st.pop('mi'), wout_ref[...], preferred_element_type=F32)
        pre1_scr[...] = ALPHA * x_ref[0] + m

    def back_ffn(c):
        cs = slice(c * FF_CHUNK, (c + 1) * FF_CHUNK)
        x1b = x1b_scr[...]
        a = jnp.dot(x1b, wg_ref[:, cs], preferred_element_type=F32)
        up = jnp.dot(x1b, wu_ref[:, cs], preferred_element_type=F32)
        a1, a2 = _conv_taps(a, conv_scr[:, cs])
        cc = a2 * cw_ref[0:1, cs] + a1 * cw_ref[1:2, cs] + a * cw_ref[2:3, cs] + cb_ref[:, cs]
        h_scr[:, cs] = (_gelu(cc) * up).astype(BF16)
        conv_scr[:, cs] = a[tile - SUBLANES:]

    def back_down(nb):
        cols = slice(nb * DOWN_BLOCK, (nb + 1) * DOWN_BLOCK)
        f = jnp.dot(h_scr[...], wd_ref[:, cols], preferred_element_type=F32)
        pre2_scr[:, cols] = ALPHA * x1c_scr[:, cols] + f

    n_down = D_MODEL // DOWN_BLOCK
    assert len(attn_blocks) == n_down
    front_dot_a()
    back_ln1()
    front_dot_b()
    out_ln2()
    front_epi_a()
    back_ffn(0)
    front_epi_b()
    back_ffn(1)
    back_ffn(2)
    front_gmlp()
    back_ffn(3)
    back_ffn(4)
    front_attn_prep()
    for c in range(5, N_FF_CHUNKS):
        back_ffn(c)
    attn_qk(0)
    for i in range(n_down):
        back_down(i)
        attn_softmax(i)
        if i + 1 < n_down:
            attn_qk(i + 1)
        attn_pv(i)
    front_rms()
    front_wout()

    @pl.when(s <= n_tiles)
    def _():
        pc_ref[0] = conv_scr[...]


def _sample_kernel(sinks_ref, x_ref, cos_ref, sin_ref, ck_ref, cv_ref, s1_ref, s2_ref,
                   win_ref, lng_ref, lnb_ref, wt_ref, bs_ref,
                   nag_ref, nbg_ref, wout_ref, ln1g_ref, ln1b_ref, wg_ref, wu_ref, cw_ref, cb_ref,
                   wd_ref, ln2g_ref, ln2b_ref,
                   y_ref, k_ref, v_ref, gv_ref, a_ref,
                   mix_scr, h_scr, *, n_batch, s_len):
    rows_all = n_batch * s_len
    x = x_ref[...]
    xb = x.astype(BF16)
    u, gv, q, k, v = _mixer_inputs(xb, win_ref, lng_ref[...], lnb_ref[...], cos_ref[...], sin_ref[...])
    gv_ref[...] = gv
    k_ref[...] = k
    v_ref[...] = v

    ri = _row_iota((rows_all, rows_all))
    ci = _lane_iota((rows_all, rows_all))
    same_batch = (ri // s_len) == (ci // s_len)
    causal = ((ri % s_len) // CHUNK) >= ((ci % s_len) // CHUNK)
    gvb = gv.astype(BF16)
    for h in range(A_HEADS):
        wm = jnp.where(same_batch & causal, wt_ref[h], 0.0).astype(BF16)
        cols = slice(h * A_HEAD_DIM, (h + 1) * A_HEAD_DIM)
        s = jnp.dot(wm, gvb[:, cols], preferred_element_type=F32) + bs_ref[h]
        mix_scr[:, cols] = u[:, cols] * s

    qb = q.astype(BF16)
    qmasks = _head_lane_masks(s_len, ATTN_SCALE)
    vmasks = _head_lane_masks(KEY_SPAN, 1.0)
    allowed = _lane_iota((s_len, KEY_SPAN)) < WINDOW + s_len
    pad = jnp.zeros((KEY_SPAN - WINDOW - s_len, LANES), F32)
    for b in range(n_batch):
        rows = slice(b * s_len, (b + 1) * s_len)
        k_all = jnp.concatenate([ck_ref[b], k[rows], pad], 0)
        v_all = jnp.concatenate([cv_ref[b], v[rows], pad], 0)
        kreps = [a.astype(BF16) for a in _replicate_kv_heads(k_all)]
        vreps = [a.astype(BF16) for a in _replicate_kv_heads(v_all)]
        for g in range(B_KV_HEADS):
            sinks = [sinks_ref[g * GQA_GROUP + h] for h in range(GQA_GROUP)]
            qcols = slice(g * Q_GROUP_WIDTH, (g + 1) * Q_GROUP_WIDTH)
            out = _attn_block(qb[rows, qcols], kreps[g], vreps[g], allowed, sinks, qmasks, vmasks)
            mix_scr[rows, A_WIDTH + g * Q_GROUP_WIDTH:A_WIDTH + (g + 1) * Q_GROUP_WIDTH] = out

    x1 = _merge(x, mix_scr[...], nag_ref[...], nbg_ref[...], wout_ref, ln1g_ref[...], ln1b_ref[...])
    x1b = x1.astype(BF16)

    pos = _row_iota((rows_all, FF_CHUNK)) % s_len
    for c in range(N_FF_CHUNKS):
        cs = slice(c * FF_CHUNK, (c + 1) * FF_CHUNK)
        a = jnp.dot(x1b, wg_ref[:, cs], preferred_element_type=F32)
        up = jnp.dot(x1b, wu_ref[:, cs], preferred_element_type=F32)
        a_ref[:, cs] = a
        a1 = jnp.where(pos < 1, s1_ref[:, cs], pltpu.roll(a, 1, 0))
        a2 = jnp.where(pos < 2, s2_ref[:, cs], pltpu.roll(a, 2, 0))
        cc = a2 * cw_ref[0:1, cs] + a1 * cw_ref[1:2, cs] + a * cw_ref[2:3, cs] + cb_ref[:, cs]
        h_scr[:, cs] = (_gelu(cc) * up).astype(BF16)
    f = jnp.dot(h_scr[...], wd_ref[...], preferred_element_type=F32)
    y_ref[...] = _layer_norm(ALPHA * x1 + f, ln2g_ref[...], ln2b_ref[...])


def _rope_tables(pos):
    inv = ROPE_THETA ** (-jnp.arange(HALF, dtype=F32) / HALF)
    ang = pos.astype(F32)[:, None] * inv[None, :]
    cos, sin = jnp.cos(ang), jnp.sin(ang)
    reps = LANES // HEAD_DIM
    return (jnp.tile(jnp.concatenate([cos, cos], -1), (1, reps)),
            jnp.tile(jnp.concatenate([-sin, sin], -1), (1, reps)))


def _resident(shape):
    return pl.BlockSpec(shape, lambda *_: (0,) * len(shape), pipeline_mode=pl.Buffered(1))


def kernel(x_prompt, x_sample, cache_k, cache_v, state_ffn_conv, w_in, gmlp_ln_g, gmlp_ln_b,
           gmlp_w_s, gmlp_b_s, attn_sinks, norm_a_g, norm_b_g, w_out, ln1_g, ln1_b,
           w_gate, w_up, conv_w, conv_b, w_down, ln2_g, ln2_b):
    assert w_in.shape[0] == DEPTH == 1
    bp, tp, _ = x_prompt.shape
    bs, ts, _ = x_sample.shape
    tile = SEQ_TILE
    assert tp % tile == 0 and tile % ROW_BLOCK == 0 and tile >= WINDOW
    assert bs * ts == ROW_BLOCK and WINDOW + ts <= KEY_SPAN and ts >= CONV_WIDTH - 1

    row = lambda a: a[0].reshape(1, -1)
    win_b, wout_b = w_in[0].astype(BF16), w_out[0].astype(BF16)
    wg_b, wu_b, wd_b = w_gate[0].astype(BF16), w_up[0].astype(BF16), w_down[0].astype(BF16)
    sinks = attn_sinks[0]
    vec_args = dict(lng=row(gmlp_ln_g), lnb=row(gmlp_ln_b), nag=row(norm_a_g), nbg=row(norm_b_g),
                    ln1g=row(ln1_g), ln1b=row(ln1_b), cb=row(conv_b), ln2g=row(ln2_g), ln2b=row(ln2_b))
    cw = conv_w[0]
    smem = pl.BlockSpec(memory_space=pltpu.SMEM)

    def weight_specs(ws_shape):
        return [_resident(win_b.shape), _resident((1, A_WIDTH)), _resident((1, A_WIDTH)),
                _resident(ws_shape), _resident(ws_shape),
                _resident((1, A_WIDTH)), _resident((1, B_WIDTH)), _resident(wout_b.shape),
                _resident((1, D_MODEL)), _resident((1, D_MODEL)),
                _resident(wg_b.shape), _resident(wu_b.shape), _resident(cw.shape), _resident((1, D_FF)),
                _resident(wd_b.shape), _resident((1, D_MODEL)), _resident((1, D_MODEL))]

    def weight_args(ws, bsb):
        return (win_b, vec_args['lng'], vec_args['lnb'], ws, bsb, vec_args['nag'], vec_args['nbg'],
                wout_b, vec_args['ln1g'], vec_args['ln1b'], wg_b, wu_b, cw, vec_args['cb'], wd_b,
                vec_args['ln2g'], vec_args['ln2b'])

    cos_p, sin_p = _rope_tables(jnp.arange(tp, dtype=jnp.int32))
    bias_p = jnp.broadcast_to(gmlp_b_s[0][:, :, None], (A_HEADS, GMLP_CHUNK, A_HEAD_DIM))
    seq_tiles = tp // tile
    n_tiles = bp * seq_tiles
    def stage_bt(lag):
        def bt(s):
            i = jnp.clip(s - lag, 0, n_tiles - 1)
            return i // seq_tiles, i % seq_tiles
        return bt
    front_bt, back_bt, out_bt = stage_bt(0), stage_bt(1), stage_bt(2)
    y_p, pk, pv, pc = pl.pallas_call(
        functools.partial(_prompt_kernel, tile=tile, seq_tiles=seq_tiles, n_tiles=n_tiles),
        grid=(n_tiles + 2,),
        in_specs=[smem,
                  pl.BlockSpec((1, tile, D_MODEL), lambda s: (*front_bt(s), 0)),
                  _resident(cos_p.shape), _resident(sin_p.shape)]
                 + weight_specs((A_HEADS, GMLP_CHUNK, GMLP_CHUNK)),
        out_specs=[pl.BlockSpec((1, tile, D_MODEL), lambda s: (*out_bt(s), 0)),
                   pl.BlockSpec((1, WINDOW, KV_WIDTH), lambda s: (front_bt(s)[0], 0, 0)),
                   pl.BlockSpec((1, WINDOW, KV_WIDTH), lambda s: (front_bt(s)[0], 0, 0)),
                   pl.BlockSpec((1, SUBLANES, D_FF), lambda s: (back_bt(s)[0], 0, 0))],
        out_shape=[jax.ShapeDtypeStruct((bp, tp, D_MODEL), F32),
                   jax.ShapeDtypeStruct((bp, WINDOW, KV_WIDTH), F32),
                   jax.ShapeDtypeStruct((bp, WINDOW, KV_WIDTH), F32),
                   jax.ShapeDtypeStruct((bp, SUBLANES, D_FF), F32)],
        scratch_shapes=[pltpu.VMEM((4, WINDOW, LANES), BF16),
                        pltpu.VMEM((SUBLANES, D_FF), F32),
                        pltpu.VMEM((tile, MIX_WIDTH), F32),
                        pltpu.VMEM((tile, D_FF), BF16),
                        pltpu.VMEM((tile, D_MODEL), F32),
                        pltpu.VMEM((tile, D_MODEL), F32),
                        pltpu.VMEM((tile, D_MODEL), F32),
                        pltpu.VMEM((tile, D_MODEL), BF16)],
        compiler_params=pltpu.CompilerParams(dimension_semantics=("arbitrary",),
                                             vmem_limit_bytes=V7X_VMEM_LIMIT_BYTES),
        name="prompt_layer",
    )(sinks, x_prompt, cos_p, sin_p, *weight_args(gmlp_w_s[0], bias_p))

    rows_all = bs * ts
    cos_s, sin_s = _rope_tables(PAST_LEN + jnp.arange(ts, dtype=jnp.int32))
    cos_s, sin_s = jnp.tile(cos_s, (bs, 1)), jnp.tile(sin_s, (bs, 1))
    ck = cache_k[0].reshape(bs, WINDOW, KV_WIDTH)
    cv = cache_v[0].reshape(bs, WINDOW, KV_WIDTH)
    st = state_ffn_conv[0]
    zeros = jnp.zeros((bs, ts, D_FF), F32)
    s1 = zeros.at[:, 0].set(st[:, 1]).reshape(rows_all, D_FF)
    s2 = zeros.at[:, 0].set(st[:, 0]).at[:, 1].set(st[:, 1]).reshape(rows_all, D_FF)
    w_tiled = jnp.tile(gmlp_w_s[0][:, :ts, :ts], (1, bs, bs))
    bias_s = jnp.broadcast_to(jnp.tile(gmlp_b_s[0][:, :ts], (1, bs))[:, :, None],
                              (A_HEADS, rows_all, A_HEAD_DIM))
    full = lambda shape: pl.BlockSpec(shape, lambda i: (0,) * len(shape))
    y_s, k_s, v_s, gv_s, a_s = pl.pallas_call(
        functools.partial(_sample_kernel, n_batch=bs, s_len=ts),
        grid=(1,),
        in_specs=[smem, _resident((rows_all, D_MODEL)), _resident(cos_s.shape), _resident(sin_s.shape),
                  _resident(ck.shape), _resident(cv.shape), _resident(s1.shape), _resident(s2.shape)]
                 + weight_specs((A_HEADS, rows_all, rows_all)),
        out_specs=[full((rows_all, D_MODEL)), full((rows_all, KV_WIDTH)), full((rows_all, KV_WIDTH)),
                   full((rows_all, A_WIDTH)), full((rows_all, D_FF))],
        out_shape=[jax.ShapeDtypeStruct((rows_all, D_MODEL), F32),
                   jax.ShapeDtypeStruct((rows_all, KV_WIDTH), F32),
                   jax.ShapeDtypeStruct((rows_all, KV_WIDTH), F32),
                   jax.ShapeDtypeStruct((rows_all, A_WIDTH), F32),
                   jax.ShapeDtypeStruct((rows_all, D_FF), F32)],
        scratch_shapes=[pltpu.VMEM((rows_all, MIX_WIDTH), F32),
                        pltpu.VMEM((rows_all, D_FF), BF16)],
        compiler_params=pltpu.CompilerParams(dimension_semantics=("arbitrary",),
                                             vmem_limit_bytes=V7X_VMEM_LIMIT_BYTES),
        name="sample_layer",
    )(sinks, x_sample.reshape(rows_all, D_MODEL), cos_s, sin_s, ck, cv, s1, s2,
      *weight_args(w_tiled, bias_s))

    kv5 = lambda a, n, t: a.reshape(1, n, t, B_KV_HEADS, HEAD_DIM)
    return (y_p, y_s.reshape(bs, ts, D_MODEL),
            kv5(pk, bp, WINDOW), kv5(pv, bp, WINDOW),
            pc[None, :, SUBLANES - (CONV_WIDTH - 1):],
            kv5(k_s, bs, ts), kv5(v_s, bs, ts),
            gv_s.reshape(1, bs, ts, A_HEADS, A_HEAD_DIM),
            a_s.reshape(bs, ts, D_FF)[None, :, ts - (CONV_WIDTH - 1):])
```

```python
import functools
import math

import jax
import jax.numpy as jnp
from jax import lax
from jax.experimental import pallas as pl
from jax.experimental.pallas import tpu as pltpu

D_MODEL = 1024
CHUNK = 64
HEAD_DIM = 64
HALF = HEAD_DIM // 2
A_HEADS = 4
A_HEAD_DIM = 128
A_WIDTH = A_HEADS * A_HEAD_DIM
GMLP_CHUNK = 128
B_HEADS = 8
B_KV_HEADS = 2
GQA_GROUP = B_HEADS // B_KV_HEADS
B_WIDTH = B_HEADS * HEAD_DIM
KV_WIDTH = B_KV_HEADS * HEAD_DIM
WINDOW = 128
MIX_WIDTH = A_WIDTH + B_WIDTH
D_FF = 2816
CONV_WIDTH = 3
PAST_LEN = 2048
ROPE_THETA = 10000.0
LN_EPS = 1e-5
RMS_EPS = 1e-6
DEPTH = 1
ALPHA = (2 * DEPTH) ** 0.25
ATTN_SCALE = HEAD_DIM ** -0.5

LANES = 128
SUBLANES = 8
V7X_VMEM_LIMIT_BYTES = 56 * 1024 * 1024

Q_GROUP_WIDTH = GQA_GROUP * HEAD_DIM
KEY_SPAN = 2 * WINDOW
ROW_BLOCK = 128
FF_CHUNK = 256
N_FF_CHUNKS = D_FF // FF_CHUNK
PROJ_BLOCK = 256
DOWN_BLOCK = 256
WOUT_BLOCK = 512
SEQ_TILE = 256

F32 = jnp.float32
BF16 = jnp.bfloat16


def _gelu(x):
    c0 = math.sqrt(2.0 / math.pi)
    c1 = c0 * 0.044715
    inner = x * (c0 + c1 * (x * x))
    hx = 0.5 * x
    return hx + hx * jnp.tanh(inner)


def _layer_norm(x, g, b):
    mu = jnp.mean(x, -1, keepdims=True)
    xc = x - mu
    var = jnp.mean(xc * xc, -1, keepdims=True)
    return xc * lax.rsqrt(var + LN_EPS) * g + b


def _rms_norm(x, g):
    ms = jnp.mean(x * x, -1, keepdims=True)
    return x * lax.rsqrt(ms + RMS_EPS) * g


def _lane_iota(shape):
    return lax.broadcasted_iota(jnp.int32, shape, 1)


def _row_iota(shape):
    return lax.broadcasted_iota(jnp.int32, shape, 0)


def _rope(x, cos, sin_signed):
    lo = (_lane_iota(cos.shape) % HEAD_DIM) < HALF
    outs = []
    for i in range(x.shape[1] // LANES):
        xs = x[:, i * LANES:(i + 1) * LANES]
        partner = jnp.where(lo, pltpu.roll(xs, LANES - HALF, 1), pltpu.roll(xs, HALF, 1))
        outs.append(xs * cos + partner * sin_signed)
    return outs[0] if len(outs) == 1 else jnp.concatenate(outs, 1)


def _replicate_kv_heads(x):
    lo = _lane_iota(x.shape) < HEAD_DIM
    sw = pltpu.roll(x, HEAD_DIM, 1)
    return jnp.where(lo, x, sw), jnp.where(lo, sw, x)


def _head_lane_masks(rows, value):
    lane = _lane_iota((rows, Q_GROUP_WIDTH))
    return [jnp.where((lane >= h * HEAD_DIM) & (lane < (h + 1) * HEAD_DIM), value, 0.0).astype(BF16)
            for h in range(GQA_GROUP)]


def _attn_scores(qg, krep, qmasks):
    k2 = jnp.concatenate([krep, krep], 1)
    qm = jnp.concatenate([qg * qmasks[h] for h in range(GQA_GROUP)], 0)
    return lax.dot_general(qm, k2, (((1,), (1,)), ((), ())), preferred_element_type=F32)


def _attn_probs(sc, allowed, sinks):
    r = sc.shape[0] // GQA_GROUP
    ps = []
    for h in range(GQA_GROUP):
        s = jnp.where(allowed, sc[h * r:(h + 1) * r], -jnp.inf)
        m = jnp.maximum(jnp.max(s, -1, keepdims=True), sinks[h])
        e = jnp.exp(s - m)
        den = jnp.sum(e, -1, keepdims=True) + jnp.exp(sinks[h] - m)
        ps.append((e * (1.0 / den)).astype(BF16))
    return jnp.concatenate(ps, 1)


def _attn_values(pc, vrep, vmasks):
    v2 = jnp.concatenate([vrep, vrep], 1)
    vm = jnp.concatenate([v2 * vmasks[h] for h in range(GQA_GROUP)], 0)
    return jnp.dot(pc, vm, preferred_element_type=F32)


def _attn_block(qg, krep, vrep, allowed, sinks, qmasks, vmasks):
    return _attn_values(_attn_probs(_attn_scores(qg, krep, qmasks), allowed, sinks), vrep, vmasks)


def _mixer_inputs(xb, win_ref, lng, lnb, cos, sin_signed):
    za = jnp.dot(xb, win_ref[:, :2 * A_WIDTH], preferred_element_type=F32)
    u = _gelu(za[:, :A_WIDTH])
    gv = _layer_norm(_gelu(za[:, A_WIDTH:]), lng, lnb)
    zb = jnp.dot(xb, win_ref[:, 2 * A_WIDTH:], preferred_element_type=F32)
    q = _rope(zb[:, :B_WIDTH], cos, sin_signed)
    k = _rope(zb[:, B_WIDTH:B_WIDTH + KV_WIDTH], cos, sin_signed)
    v = zb[:, B_WIDTH + KV_WIDTH:]
    return u, gv, q, k, v


def _merge(x, mix, nag, nbg, wout_ref, ln1g, ln1b):
    mi = jnp.concatenate([_rms_norm(mix[:, :A_WIDTH], nag), _rms_norm(mix[:, A_WIDTH:], nbg)], 1)
    m = jnp.dot(mi.astype(BF16), wout_ref[...], preferred_element_type=F32)
    return _layer_norm(ALPHA * x + m, ln1g, ln1b)


def _conv_taps(a, prev8):
    r1 = pltpu.roll(a, 1, 0)
    r2 = pltpu.roll(a, 2, 0)
    row = _row_iota(prev8.shape)
    first1 = jnp.where(row < 1, pltpu.roll(prev8, 1, 0), r1[:SUBLANES])
    first2 = jnp.where(row < 2, pltpu.roll(prev8, 2, 0), r2[:SUBLANES])
    return (jnp.concatenate([first1, r1[SUBLANES:]], 0), jnp.concatenate([first2, r2[SUBLANES:]], 0))


def _prompt_kernel(sinks_ref, x_ref, cos_ref, sin_ref, win_ref, lng_ref, lnb_ref, ws_ref, bs_ref,
                   nag_ref, nbg_ref, wout_ref, ln1g_ref, ln1b_ref, wg_ref, wu_ref, cw_ref, cb_ref,
                   wd_ref, ln2g_ref, ln2b_ref,
                   y_ref, pk_ref, pv_ref, pc_ref,
                   kv_scr, conv_scr, mix_scr, h_scr, pre1_scr, x1c_scr, x1b_scr, xb_scr, mib_scr,
                   *, tile, seq_tiles, n_tiles):
    s = pl.program_id(0)
    tm = jnp.minimum(s, n_tiles - 1) % seq_tiles
    tb = jnp.maximum(s - 1, 0) % seq_tiles

    @pl.when(s == 0)
    def _():
        pre1_scr[...] = jnp.zeros_like(pre1_scr)

    @pl.when(tm == 0)
    def _():
        kv_scr[...] = jnp.zeros_like(kv_scr)

    @pl.when(tb == 0)
    def _():
        conv_scr[...] = jnp.zeros_like(conv_scr)

    st = {}
    row_blocks = [slice(j * ROW_BLOCK, (j + 1) * ROW_BLOCK) for j in range(tile // ROW_BLOCK)]
    n_proj = win_ref.shape[1] // PROJ_BLOCK

    def proj_cast():
        xb_scr[...] = x_ref[0].astype(BF16)

    def proj_dot(i):
        cols = slice(i * PROJ_BLOCK, (i + 1) * PROJ_BLOCK)
        st['z', i] = jnp.dot(xb_scr[...], win_ref[:, cols], preferred_element_type=F32)

    def epi_u(i):
        st['u', i] = _gelu(st.pop(('z', i)))

    def epi_gv():
        g = jnp.concatenate([_gelu(st.pop(('z', 2))), _gelu(st.pop(('z', 3)))], 1)
        st['gvb'] = _layer_norm(g, lng_ref[...], lnb_ref[...]).astype(BF16)

    def rope_tables():
        row0 = pl.multiple_of(tm * tile, tile)
        return cos_ref[pl.ds(row0, tile), :], sin_ref[pl.ds(row0, tile), :]

    def epi_q(i):
        cos, sin_signed = rope_tables()
        st['qb', i - 4] = _rope(st.pop(('z', i)), cos, sin_signed).astype(BF16)

    def epi_kv():
        cos, sin_signed = rope_tables()
        zkv = st.pop(('z', 6))
        k = _rope(zkv[:, :KV_WIDTH], cos, sin_signed)
        v = zkv[:, KV_WIDTH:]
        pk_ref[0] = k[tile - WINDOW:]
        pv_ref[0] = v[tile - WINDOW:]
        k0, k1 = _replicate_kv_heads(k)
        v0, v1 = _replicate_kv_heads(v)
        reps = [jnp.concatenate([kv_scr[i], a.astype(BF16)], 0) for i, a in enumerate((k0, k1, v0, v1))]
        for i in range(4):
            kv_scr[i] = reps[i][tile:]
        st['reps'] = reps

    def gmlp():
        cidx_r = _row_iota((GMLP_CHUNK, GMLP_CHUNK)) // CHUNK
        cidx_c = _lane_iota((GMLP_CHUNK, GMLP_CHUNK)) // CHUNK
        gvb = st.pop('gvb')
        u = jnp.concatenate([st.pop(('u', 0)), st.pop(('u', 1))], 1)
        for h in range(A_HEADS):
            wm = jnp.where(cidx_r >= cidx_c, ws_ref[h], 0.0).astype(BF16)
            cols = slice(h * A_HEAD_DIM, (h + 1) * A_HEAD_DIM)
            for c in range(tile // GMLP_CHUNK):
                rows = slice(c * GMLP_CHUNK, (c + 1) * GMLP_CHUNK)
                sg = jnp.dot(wm, gvb[rows, cols], preferred_element_type=F32) + bs_ref[h]
                mix_scr[rows, cols] = u[rows, cols] * sg

    def attn_prep():
        qc = _row_iota((ROW_BLOCK, KEY_SPAN)) // CHUNK
        kc = _lane_iota((ROW_BLOCK, KEY_SPAN)) // CHUNK
        in_band = (kc >= qc) & (kc <= qc + WINDOW // CHUNK)
        first_lo = jnp.where(tm > 0, 0, WINDOW // CHUNK)
        st.update(in_band=in_band, allowed_first=in_band & (kc >= first_lo),
                  qmasks=_head_lane_masks(ROW_BLOCK, ATTN_SCALE), vmasks=_head_lane_masks(KEY_SPAN, 1.0))

    attn_blocks = [(j, g) for j in range(tile // ROW_BLOCK) for g in range(B_KV_HEADS)]

    def attn_qk(i):
        j, g = attn_blocks[i]
        qg = st['qb', g][row_blocks[j], :]
        st['sc', i] = _attn_scores(qg, st['reps'][g][j * ROW_BLOCK:j * ROW_BLOCK + KEY_SPAN], st['qmasks'])

    def attn_softmax(i):
        j, g = attn_blocks[i]
        sinks = [sinks_ref[g * GQA_GROUP + h] for h in range(GQA_GROUP)]
        allowed = st['allowed_first'] if j == 0 else st['in_band']
        st['p', i] = _attn_probs(st.pop(('sc', i)), allowed, sinks)

    def attn_pv(i):
        j, g = attn_blocks[i]
        out = _attn_values(st.pop(('p', i)), st['reps'][2 + g][j * ROW_BLOCK:j * ROW_BLOCK + KEY_SPAN],
                           st['vmasks'])
        mix_scr[row_blocks[j], A_WIDTH + g * Q_GROUP_WIDTH:A_WIDTH + (g + 1) * Q_GROUP_WIDTH] = out

    def rms(j):
        rows = row_blocks[j]
        mi = jnp.concatenate([_rms_norm(mix_scr[rows, :A_WIDTH], nag_ref[...]),
                              _rms_norm(mix_scr[rows, A_WIDTH:], nbg_ref[...])], 1)
        mib_scr[rows, :] = mi.astype(BF16)

    def wout(nb):
        cols = slice(nb * WOUT_BLOCK, (nb + 1) * WOUT_BLOCK)
        m = jnp.dot(mib_scr[...], wout_ref[:, cols], preferred_element_type=F32)
        pre1_scr[:, cols] = ALPHA * x_ref[0, :, cols] + m

    def ln1(j):
        rows = row_blocks[j]
        x1 = _layer_norm(pre1_scr[rows, :], ln1g_ref[...], ln1b_ref[...])
        x1c_scr[rows, :] = x1
        x1b_scr[rows, :] = x1.astype(BF16)

    def ffn_dots(c):
        cs = slice(c * FF_CHUNK, (c + 1) * FF_CHUNK)
        x1b = x1b_scr[...]
        st['a', c] = jnp.dot(x1b, wg_ref[:, cs], preferred_element_type=F32)
        st['up', c] = jnp.dot(x1b, wu_ref[:, cs], preferred_element_type=F32)

    def ffn_epi(c):
        cs = slice(c * FF_CHUNK, (c + 1) * FF_CHUNK)
        a, up = st.pop(('a', c)), st.pop(('up', c))
        a1, a2 = _conv_taps(a, conv_scr[:, cs])
        cc = a2 * cw_ref[0:1, cs] + a1 * cw_ref[1:2, cs] + a * cw_ref[2:3, cs] + cb_ref[:, cs]
        h_scr[:, cs] = (_gelu(cc) * up).astype(BF16)
        conv_scr[:, cs] = a[tile - SUBLANES:]

    def down(nb):
        cols = slice(nb * DOWN_BLOCK, (nb + 1) * DOWN_BLOCK)
        f = jnp.dot(h_scr[...], wd_ref[:, cols], preferred_element_type=F32)
        st['pre2', nb] = ALPHA * x1c_scr[:, cols] + f

    def ln2():
        pre2 = jnp.concatenate([st.pop(('pre2', nb)) for nb in range(D_MODEL // DOWN_BLOCK)], 1)
        y_ref[0] = _layer_norm(pre2, ln2g_ref[...], ln2b_ref[...])

    assert n_proj == 7 and len(row_blocks) == 2 and len(attn_blocks) == 4 and N_FF_CHUNKS == 11
    proj_cast()
    proj_dot(0)
    ln1(0)
    proj_dot(1)
    ln1(1)
    proj_dot(2)
    ffn_dots(0)
    proj_dot(3)
    epi_u(0)
    ffn_dots(1)
    ffn_epi(0)
    proj_dot(4)
    epi_u(1)
    ffn_dots(2)
    ffn_epi(1)
    proj_dot(5)
    epi_gv()
    ffn_dots(3)
    ffn_epi(2)
    proj_dot(6)
    epi_q(4)
    ffn_dots(4)
    ffn_epi(3)
    gmlp()
    epi_q(5)
    ffn_dots(5)
    ffn_epi(4)
    epi_kv()
    attn_prep()
    for c in range(6, N_FF_CHUNKS):
        ffn_dots(c)
        ffn_epi(c - 1)
    attn_qk(0)
    attn_qk(1)
    ffn_epi(N_FF_CHUNKS - 1)
    down(0)
    attn_softmax(0)
    attn_qk(2)
    attn_qk(3)
    attn_softmax(1)
    down(1)
    attn_pv(0)
    attn_pv(1)
    attn_softmax(2)
    rms(0)
    down(2)
    attn_softmax(3)
    attn_pv(2)
    attn_pv(3)
    down(3)
    rms(1)
    wout(0)
    ln2()
    wout(1)
    pc_ref[0] = conv_scr[...]


def _sample_kernel(sinks_ref, x_ref, cos_ref, sin_ref, ck_ref, cv_ref, s1_ref, s2_ref,
                   win_ref, lng_ref, lnb_ref, wt_ref, bs_ref,
                   nag_ref, nbg_ref, wout_ref, ln1g_ref, ln1b_ref, wg_ref, wu_ref, cw_ref, cb_ref,
                   wd_ref, ln2g_ref, ln2b_ref,
                   y_ref, k_ref, v_ref, gv_ref, a_ref,
                   mix_scr, h_scr, *, n_batch, s_len):
    rows_all = n_batch * s_len
    x = x_ref[...]
    xb = x.astype(BF16)
    u, gv, q, k, v = _mixer_inputs(xb, win_ref, lng_ref[...], lnb_ref[...], cos_ref[...], sin_ref[...])
    gv_ref[...] = gv
    k_ref[...] = k
    v_ref[...] = v

    ri = _row_iota((rows_all, rows_all))
    ci = _lane_iota((rows_all, rows_all))
    same_batch = (ri // s_len) == (ci // s_len)
    causal = ((ri % s_len) // CHUNK) >= ((ci % s_len) // CHUNK)
    gvb = gv.astype(BF16)
    for h in range(A_HEADS):
        wm = jnp.where(same_batch & causal, wt_ref[h], 0.0).astype(BF16)
        cols = slice(h * A_HEAD_DIM, (h + 1) * A_HEAD_DIM)
        s = jnp.dot(wm, gvb[:, cols], preferred_element_type=F32) + bs_ref[h]
        mix_scr[:, cols] = u[:, cols] * s

    qb = q.astype(BF16)
    qmasks = _head_lane_masks(s_len, ATTN_SCALE)
    vmasks = _head_lane_masks(KEY_SPAN, 1.0)
    allowed = _lane_iota((s_len, KEY_SPAN)) < WINDOW + s_len
    pad = jnp.zeros((KEY_SPAN - WINDOW - s_len, LANES), F32)
    for b in range(n_batch):
        rows = slice(b * s_len, (b + 1) * s_len)
        k_all = jnp.concatenate([ck_ref[b], k[rows], pad], 0)
        v_all = jnp.concatenate([cv_ref[b], v[rows], pad], 0)
        kreps = [a.astype(BF16) for a in _replicate_kv_heads(k_all)]
        vreps = [a.astype(BF16) for a in _replicate_kv_heads(v_all)]
        for g in range(B_KV_HEADS):
            sinks = [sinks_ref[g * GQA_GROUP + h] for h in range(GQA_GROUP)]
            qcols = slice(g * Q_GROUP_WIDTH, (g + 1) * Q_GROUP_WIDTH)
            out = _attn_block(qb[rows, qcols], kreps[g], vreps[g], allowed, sinks, qmasks, vmasks)
            mix_scr[rows, A_WIDTH + g * Q_GROUP_WIDTH:A_WIDTH + (g + 1) * Q_GROUP_WIDTH] = out

    x1 = _merge(x, mix_scr[...], nag_ref[...], nbg_ref[...], wout_ref, ln1g_ref[...], ln1b_ref[...])
    x1b = x1.astype(BF16)

    pos = _row_iota((rows_all, FF_CHUNK)) % s_len
    for c in range(N_FF_CHUNKS):
        cs = slice(c * FF_CHUNK, (c + 1) * FF_CHUNK)
        a = jnp.dot(x1b, wg_ref[:, cs], preferred_element_type=F32)
        up = jnp.dot(x1b, wu_ref[:, cs], preferred_element_type=F32)
        a_ref[:, cs] = a
        a1 = jnp.where(pos < 1, s1_ref[:, cs], pltpu.roll(a, 1, 0))
        a2 = jnp.where(pos < 2, s2_ref[:, cs], pltpu.roll(a, 2, 0))
        cc = a2 * cw_ref[0:1, cs] + a1 * cw_ref[1:2, cs] + a * cw_ref[2:3, cs] + cb_ref[:, cs]
        h_scr[:, cs] = (_gelu(cc) * up).astype(BF16)
    f = jnp.dot(h_scr[...], wd_ref[...], preferred_element_type=F32)
    y_ref[...] = _layer_norm(ALPHA * x1 + f, ln2g_ref[...], ln2b_ref[...])


def _rope_tables(pos):
    inv = ROPE_THETA ** (-jnp.arange(HALF, dtype=F32) / HALF)
    ang = pos.astype(F32)[:, None] * inv[None, :]
    cos, sin = jnp.cos(ang), jnp.sin(ang)
    reps = LANES // HEAD_DIM
    return (jnp.tile(jnp.concatenate([cos, cos], -1), (1, reps)),
            jnp.tile(jnp.concatenate([-sin, sin], -1), (1, reps)))


def _resident(shape):
    return pl.BlockSpec(shape, lambda *_: (0,) * len(shape), pipeline_mode=pl.Buffered(1))


def kernel(x_prompt, x_sample, cache_k, cache_v, state_ffn_conv, w_in, gmlp_ln_g, gmlp_ln_b,
           gmlp_w_s, gmlp_b_s, attn_sinks, norm_a_g, norm_b_g, w_out, ln1_g, ln1_b,
           w_gate, w_up, conv_w, conv_b, w_down, ln2_g, ln2_b):
    assert w_in.shape[0] == DEPTH == 1
    bp, tp, _ = x_prompt.shape
    bs, ts, _ = x_sample.shape
    tile = SEQ_TILE
    assert tp % tile == 0 and tile % ROW_BLOCK == 0 and tile >= WINDOW
    assert bs * ts == ROW_BLOCK and WINDOW + ts <= KEY_SPAN and ts >= CONV_WIDTH - 1

    row = lambda a: a[0].reshape(1, -1)
    win_b, wout_b = w_in[0].astype(BF16), w_out[0].astype(BF16)
    wg_b, wu_b, wd_b = w_gate[0].astype(BF16), w_up[0].astype(BF16), w_down[0].astype(BF16)
    sinks = attn_sinks[0]
    vec_args = dict(lng=row(gmlp_ln_g), lnb=row(gmlp_ln_b), nag=row(norm_a_g), nbg=row(norm_b_g),
                    ln1g=row(ln1_g), ln1b=row(ln1_b), cb=row(conv_b), ln2g=row(ln2_g), ln2b=row(ln2_b))
    cw = conv_w[0]
    smem = pl.BlockSpec(memory_space=pltpu.SMEM)

    def weight_specs(ws_shape):
        return [_resident(win_b.shape), _resident((1, A_WIDTH)), _resident((1, A_WIDTH)),
                _resident(ws_shape), _resident(ws_shape),
                _resident((1, A_WIDTH)), _resident((1, B_WIDTH)), _resident(wout_b.shape),
                _resident((1, D_MODEL)), _resident((1, D_MODEL)),
                _resident(wg_b.shape), _resident(wu_b.shape), _resident(cw.shape), _resident((1, D_FF)),
                _resident(wd_b.shape), _resident((1, D_MODEL)), _resident((1, D_MODEL))]

    def weight_args(ws, bsb):
        return (win_b, vec_args['lng'], vec_args['lnb'], ws, bsb, vec_args['nag'], vec_args['nbg'],
                wout_b, vec_args['ln1g'], vec_args['ln1b'], wg_b, wu_b, cw, vec_args['cb'], wd_b,
                vec_args['ln2g'], vec_args['ln2b'])

    cos_p, sin_p = _rope_tables(jnp.arange(tp, dtype=jnp.int32))
    bias_p = jnp.broadcast_to(gmlp_b_s[0][:, :, None], (A_HEADS, GMLP_CHUNK, A_HEAD_DIM))
    seq_tiles = tp // tile
    n_tiles = bp * seq_tiles
    def stage_bt(lag):
        def bt(s):
            i = jnp.clip(s - lag, 0, n_tiles - 1)
            return i // seq_tiles, i % seq_tiles
        return bt
    mixer_bt, ffn_bt = stage_bt(0), stage_bt(1)
    y_p, pk, pv, pc = pl.pallas_call(
        functools.partial(_prompt_kernel, tile=tile, seq_tiles=seq_tiles, n_tiles=n_tiles),
        grid=(n_tiles + 1,),
        in_specs=[smem,
                  pl.BlockSpec((1, tile, D_MODEL), lambda s: (*mixer_bt(s), 0)),
                  _resident(cos_p.shape), _resident(sin_p.shape)]
                 + weight_specs((A_HEADS, GMLP_CHUNK, GMLP_CHUNK)),
        out_specs=[pl.BlockSpec((1, tile, D_MODEL), lambda s: (*ffn_bt(s), 0)),
                   pl.BlockSpec((1, WINDOW, KV_WIDTH), lambda s: (mixer_bt(s)[0], 0, 0)),
                   pl.BlockSpec((1, WINDOW, KV_WIDTH), lambda s: (mixer_bt(s)[0], 0, 0)),
                   pl.BlockSpec((1, SUBLANES, D_FF), lambda s: (ffn_bt(s)[0], 0, 0))],
        out_shape=[jax.ShapeDtypeStruct((bp, tp, D_MODEL), F32),
                   jax.ShapeDtypeStruct((bp, WINDOW, KV_WIDTH), F32),
                   jax.ShapeDtypeStruct((bp, WINDOW, KV_WIDTH), F32),
                   jax.ShapeDtypeStruct((bp, SUBLANES, D_FF), F32)],
        scratch_shapes=[pltpu.VMEM((4, WINDOW, LANES), BF16),
                        pltpu.VMEM((SUBLANES, D_FF), F32),
                        pltpu.VMEM((tile, MIX_WIDTH), F32),
                        pltpu.VMEM((tile, D_FF), BF16),
                        pltpu.VMEM((tile, D_MODEL), F32),
                        pltpu.VMEM((tile, D_MODEL), F32),
                        pltpu.VMEM((tile, D_MODEL), BF16),
                        pltpu.VMEM((tile, D_MODEL), BF16),
                        pltpu.VMEM((tile, MIX_WIDTH), BF16)],
        compiler_params=pltpu.CompilerParams(dimension_semantics=("arbitrary",),
                                             vmem_limit_bytes=V7X_VMEM_LIMIT_BYTES),
        name="prompt_layer",
    )(sinks, x_prompt, cos_p, sin_p, *weight_args(gmlp_w_s[0], bias_p))

    rows_all = bs * ts
    cos_s, sin_s = _rope_tables(PAST_LEN + jnp.arange(ts, dtype=jnp.int32))
    cos_s, sin_s = jnp.tile(cos_s, (bs, 1)), jnp.tile(sin_s, (bs, 1))
    ck = cache_k[0].reshape(bs, WINDOW, KV_WIDTH)
    cv = cache_v[0].reshape(bs, WINDOW, KV_WIDTH)
    st = state_ffn_conv[0]
    zeros = jnp.zeros((bs, ts, D_FF), F32)
    s1 = zeros.at[:, 0].set(st[:, 1]).reshape(rows_all, D_FF)
    s2 = zeros.at[:, 0].set(st[:, 0]).at[:, 1].set(st[:, 1]).reshape(rows_all, D_FF)
    w_tiled = jnp.tile(gmlp_w_s[0][:, :ts, :ts], (1, bs, bs))
    bias_s = jnp.broadcast_to(jnp.tile(gmlp_b_s[0][:, :ts], (1, bs))[:, :, None],
                              (A_HEADS, rows_all, A_HEAD_DIM))
    full = lambda shape: pl.BlockSpec(shape, lambda i: (0,) * len(shape))
    y_s, k_s, v_s, gv_s, a_s = pl.pallas_call(
        functools.partial(_sample_kernel, n_batch=bs, s_len=ts),
        grid=(1,),
        in_specs=[smem, _resident((rows_all, D_MODEL)), _resident(cos_s.shape), _resident(sin_s.shape),
                  _resident(ck.shape), _resident(cv.shape), _resident(s1.shape), _resident(s2.shape)]
                 + weight_specs((A_HEADS, rows_all, rows_all)),
        out_specs=[full((rows_all, D_MODEL)), full((rows_all, KV_WIDTH)), full((rows_all, KV_WIDTH)),
                   full((rows_all, A_WIDTH)), full((rows_all, D_FF))],
        out_shape=[jax.ShapeDtypeStruct((rows_all, D_MODEL), F32),
                   jax.ShapeDtypeStruct((rows_all, KV_WIDTH), F32),
                   jax.ShapeDtypeStruct((rows_all, KV_WIDTH), F32),
                   jax.ShapeDtypeStruct((rows_all, A_WIDTH), F32),
                   jax.ShapeDtypeStruct((rows_all, D_FF), F32)],
        scratch_shapes=[pltpu.VMEM((rows_all, MIX_WIDTH), F32),
                        pltpu.VMEM((rows_all, D_FF), BF16)],
        compiler_params=pltpu.CompilerParams(dimension_semantics=("arbitrary",),
                                             vmem_limit_bytes=V7X_VMEM_LIMIT_BYTES),
        name="sample_layer",
    )(sinks, x_sample.reshape(rows_all, D_MODEL), cos_s, sin_s, ck, cv, s1, s2,
      *weight_args(w_tiled, bias_s))

    kv5 = lambda a, n, t: a.reshape(1, n, t, B_KV_HEADS, HEAD_DIM)
    return (y_p, y_s.reshape(bs, ts, D_MODEL),
            kv5(pk, bp, WINDOW), kv5(pv, bp, WINDOW),
            pc[None, :, SUBLANES - (CONV_WIDTH - 1):],
            kv5(k_s, bs, ts), kv5(v_s, bs, ts),
            gv_s.reshape(1, bs, ts, A_HEADS, A_HEAD_DIM),
            a_s.reshape(bs, ts, D_FF)[None, :, ts - (CONV_WIDTH - 1):])
```

```python
import functools
import math

import jax
import jax.numpy as jnp
import numpy as np
from jax import lax
from jax.experimental import pallas as pl
from jax.experimental.pallas import tpu as pltpu

D_MODEL = 1024
CHUNK = 64
HEAD_DIM = 64
HALF = HEAD_DIM // 2
A_HEADS = 4
A_HEAD_DIM = 128
A_WIDTH = A_HEADS * A_HEAD_DIM
GMLP_CHUNK = 128
B_HEADS = 8
B_KV_HEADS = 2
GQA_GROUP = B_HEADS // B_KV_HEADS
B_WIDTH = B_HEADS * HEAD_DIM
KV_WIDTH = B_KV_HEADS * HEAD_DIM
WINDOW = 128
MIX_WIDTH = A_WIDTH + B_WIDTH
D_FF = 2816
CONV_WIDTH = 3
PAST_LEN = 2048
ROPE_THETA = 10000.0
LN_EPS = 1e-5
RMS_EPS = 1e-6
DEPTH = 1
ALPHA = (2 * DEPTH) ** 0.25
ATTN_SCALE = HEAD_DIM ** -0.5
LOG2_E = math.log2(math.e)
QK_SCALE = ATTN_SCALE * LOG2_E

LANES = 128
SUBLANES = 8
V7X_VMEM_LIMIT_BYTES = 56 * 1024 * 1024

Q_GROUP_WIDTH = GQA_GROUP * HEAD_DIM
KEY_SPAN = 2 * WINDOW
ROW_BLOCK = 128
FF_CHUNK = 256
N_FF_CHUNKS = D_FF // FF_CHUNK
PROJ_BLOCK = 256
DOWN_BLOCK = 256
WOUT_BLOCK = 512
SEQ_TILE = 256

F32 = jnp.float32
BF16 = jnp.bfloat16


def _gelu(x):
    k0 = -2.0 * math.sqrt(2.0 / math.pi) * math.log2(math.e)
    k1 = k0 * 0.044715
    return x * (1.0 / (1.0 + jnp.exp2(x * (k0 + k1 * (x * x)))))


def _layer_norm(x, g, b):
    mu = jnp.mean(x, -1, keepdims=True)
    xc = x - mu
    var = jnp.mean(xc * xc, -1, keepdims=True)
    return xc * lax.rsqrt(var + LN_EPS) * g + b


def _rms_norm(x, g):
    ms = jnp.mean(x * x, -1, keepdims=True)
    return x * lax.rsqrt(ms + RMS_EPS) * g


def _lane_iota(shape):
    return lax.broadcasted_iota(jnp.int32, shape, 1)


def _row_iota(shape):
    return lax.broadcasted_iota(jnp.int32, shape, 0)


def _rope(x, cos, sin_signed):
    lo = (_lane_iota(cos.shape) % HEAD_DIM) < HALF
    outs = []
    for i in range(x.shape[1] // LANES):
        xs = x[:, i * LANES:(i + 1) * LANES]
        partner = jnp.where(lo, pltpu.roll(xs, LANES - HALF, 1), pltpu.roll(xs, HALF, 1))
        outs.append(xs * cos + partner * sin_signed)
    return outs[0] if len(outs) == 1 else jnp.concatenate(outs, 1)


def _replicate_kv_heads(x):
    lo = _lane_iota(x.shape) < HEAD_DIM
    sw = pltpu.roll(x, HEAD_DIM, 1)
    return jnp.where(lo, x, sw), jnp.where(lo, sw, x)


def _head_lane_masks(rows):
    lane = _lane_iota((rows, Q_GROUP_WIDTH))
    return [jnp.where((lane >= h * HEAD_DIM) & (lane < (h + 1) * HEAD_DIM), 1.0, 0.0).astype(BF16)
            for h in range(GQA_GROUP)]


def _attn_scores(qg, krep, qmasks):
    k2 = jnp.concatenate([krep, krep], 1)
    qm = jnp.concatenate([qg * qmasks[h] for h in range(GQA_GROUP)], 0)
    return lax.dot_general(qm, k2, (((1,), (1,)), ((), ())), preferred_element_type=F32)


def _attn_probs(sc, allowed, sinks):
    r = sc.shape[0] // GQA_GROUP
    ps = []
    for h in range(GQA_GROUP):
        s = jnp.where(allowed, sc[h * r:(h + 1) * r], -jnp.inf)
        sink = sinks[h] * LOG2_E
        m = jnp.maximum(jnp.max(s, -1, keepdims=True), sink)
        e = jnp.exp2(s - m)
        den = jnp.sum(e, -1, keepdims=True) + jnp.exp2(sink - m)
        ps.append((e * (1.0 / den)).astype(BF16))
    return jnp.concatenate(ps, 1)


def _attn_values(pc, vrep, vmasks):
    v2 = jnp.concatenate([vrep, vrep], 1)
    vm = jnp.concatenate([v2 * vmasks[h] for h in range(GQA_GROUP)], 0)
    return jnp.dot(pc, vm, preferred_element_type=F32)


def _attn_block(qg, krep, vrep, allowed, sinks, qmasks, vmasks):
    return _attn_values(_attn_probs(_attn_scores(qg, krep, qmasks), allowed, sinks), vrep, vmasks)


def _mixer_inputs(xb, win_ref, lng, lnb, cos, sin_signed):
    za = jnp.dot(xb, win_ref[:, :2 * A_WIDTH], preferred_element_type=F32)
    u = _gelu(za[:, :A_WIDTH])
    gv = _layer_norm(_gelu(za[:, A_WIDTH:]), lng, lnb)
    zb = jnp.dot(xb, win_ref[:, 2 * A_WIDTH:], preferred_element_type=F32)
    qs = _rope(zb[:, :B_WIDTH], cos * QK_SCALE, sin_signed * QK_SCALE)
    k = _rope(zb[:, B_WIDTH:B_WIDTH + KV_WIDTH], cos, sin_signed)
    v = zb[:, B_WIDTH + KV_WIDTH:]
    return u, gv, qs, k, v


def _merge(x, mix, nag, nbg, wout_ref, ln1g, ln1b):
    mi = jnp.concatenate([_rms_norm(mix[:, :A_WIDTH], nag), _rms_norm(mix[:, A_WIDTH:], nbg)], 1)
    m = jnp.dot(mi.astype(BF16), wout_ref[:, :D_MODEL], preferred_element_type=F32)
    return _layer_norm(ALPHA * x + m, ln1g, ln1b)


def _conv_taps(a, prev8):
    r1 = pltpu.roll(a, 1, 0)
    r2 = pltpu.roll(a, 2, 0)
    row = _row_iota(prev8.shape)
    first1 = jnp.where(row < 1, pltpu.roll(prev8, 1, 0), r1[:SUBLANES])
    first2 = jnp.where(row < 2, pltpu.roll(prev8, 2, 0), r2[:SUBLANES])
    return (jnp.concatenate([first1, r1[SUBLANES:]], 0), jnp.concatenate([first2, r2[SUBLANES:]], 0))


def _prompt_kernel(sinks_ref, x_ref, cos_ref, sin_ref, win_ref, lng_ref, lnb_ref, ws_ref, bs_ref,
                   nag_ref, nbg_ref, wout_ref, ln1g_ref, ln1b_ref, wg_ref, wu_ref, cw_ref, cb_ref,
                   wd_ref, ln2g_ref, ln2b_ref,
                   y_ref, pk_ref, pv_ref, pc_ref,
                   kv_scr, conv_scr, mix_scr, h_scr, pre1_scr, x1c_scr, x1b_scr, xb_scr, mib_scr,
                   *, tile, seq_tiles, n_tiles):
    s = pl.program_id(0)
    tm = jnp.minimum(s, n_tiles - 1) % seq_tiles
    tb = jnp.maximum(s - 1, 0) % seq_tiles

    @pl.when(s == 0)
    def _():
        pre1_scr[...] = jnp.zeros_like(pre1_scr)

    @pl.when(tm == 0)
    def _():
        kv_scr[...] = jnp.zeros_like(kv_scr)

    @pl.when(tb == 0)
    def _():
        conv_scr[...] = jnp.zeros_like(conv_scr)

    st = {}
    row_blocks = [slice(j * ROW_BLOCK, (j + 1) * ROW_BLOCK) for j in range(tile // ROW_BLOCK)]
    n_proj = win_ref.shape[1] // PROJ_BLOCK

    def proj_cast():
        xb_scr[...] = x_ref[0].astype(BF16)

    def proj_dot(i):
        cols = slice(i * PROJ_BLOCK, (i + 1) * PROJ_BLOCK)
        st['z', i] = jnp.dot(xb_scr[...], win_ref[:, cols], preferred_element_type=F32)

    def epi_u(i):
        st['u', i] = _gelu(st.pop(('z', i)))

    def epi_gv():
        g = jnp.concatenate([_gelu(st.pop(('z', 2))), _gelu(st.pop(('z', 3)))], 1)
        st['gvb'] = _layer_norm(g, lng_ref[...], lnb_ref[...]).astype(BF16)

    def rope_tables():
        row0 = pl.multiple_of(tm * tile, tile)
        return cos_ref[pl.ds(row0, tile), :], sin_ref[pl.ds(row0, tile), :]

    def epi_q(i):
        cos, sin_signed = rope_tables()
        st['qb', i - 4] = _rope(st.pop(('z', i)), cos * QK_SCALE, sin_signed * QK_SCALE).astype(BF16)

    def epi_kv():
        cos, sin_signed = rope_tables()
        zkv = st.pop(('z', 6))
        k = _rope(zkv[:, :KV_WIDTH], cos, sin_signed)
        v = zkv[:, KV_WIDTH:]
        pk_ref[0] = k[tile - WINDOW:]
        pv_ref[0] = v[tile - WINDOW:]
        k0, k1 = _replicate_kv_heads(k)
        v0, v1 = _replicate_kv_heads(v)
        reps = [jnp.concatenate([kv_scr[i], a.astype(BF16)], 0) for i, a in enumerate((k0, k1, v0, v1))]
        for i in range(4):
            kv_scr[i] = reps[i][tile:]
        st['reps'] = reps

    def gmlp():
        cidx_r = _row_iota((GMLP_CHUNK, GMLP_CHUNK)) // CHUNK
        cidx_c = _lane_iota((GMLP_CHUNK, GMLP_CHUNK)) // CHUNK
        gvb = st.pop('gvb')
        zero = jnp.zeros((GMLP_CHUNK, A_HEAD_DIM), BF16)
        for hp in range(A_HEADS // 2):
            ha, hb = 2 * hp, 2 * hp + 1
            wm = jnp.concatenate([jnp.where(cidx_r >= cidx_c, ws_ref[h], 0.0) for h in (ha, hb)], 1).astype(BF16)
            bias = jnp.concatenate([bs_ref[ha], bs_ref[hb]], 1)
            cols = slice(ha * A_HEAD_DIM, (hb + 1) * A_HEAD_DIM)
            u = st.pop(('u', hp))
            for c in range(tile // GMLP_CHUNK):
                rows = slice(c * GMLP_CHUNK, (c + 1) * GMLP_CHUNK)
                ga = gvb[rows, ha * A_HEAD_DIM:(ha + 1) * A_HEAD_DIM]
                gb = gvb[rows, hb * A_HEAD_DIM:(hb + 1) * A_HEAD_DIM]
                rhs = jnp.concatenate([jnp.concatenate([ga, zero], 1), jnp.concatenate([zero, gb], 1)], 0)
                sg = jnp.dot(wm, rhs, preferred_element_type=F32) + bias
                mix_scr[rows, cols] = u[rows, :] * sg

    def attn_prep():
        qc = _row_iota((ROW_BLOCK, KEY_SPAN)) // CHUNK
        kc = _lane_iota((ROW_BLOCK, KEY_SPAN)) // CHUNK
        in_band = (kc >= qc) & (kc <= qc + WINDOW // CHUNK)
        first_lo = jnp.where(tm > 0, 0, WINDOW // CHUNK)
        st.update(in_band=in_band, allowed_first=in_band & (kc >= first_lo),
                  qmasks=_head_lane_masks(ROW_BLOCK), vmasks=_head_lane_masks(KEY_SPAN))

    attn_blocks = [(j, g) for j in range(tile // ROW_BLOCK) for g in range(B_KV_HEADS)]

    def attn_qk(i):
        j, g = attn_blocks[i]
        qg = st['qb', g][row_blocks[j], :]
        st['sc', i] = _attn_scores(qg, st['reps'][g][j * ROW_BLOCK:j * ROW_BLOCK + KEY_SPAN], st['qmasks'])

    def attn_softmax(i):
        j, g = attn_blocks[i]
        sinks = [sinks_ref[g * GQA_GROUP + h] for h in range(GQA_GROUP)]
        allowed = st['allowed_first'] if j == 0 else st['in_band']
        st['p', i] = _attn_probs(st.pop(('sc', i)), allowed, sinks)

    def attn_pv(i):
        j, g = attn_blocks[i]
        out = _attn_values(st.pop(('p', i)), st['reps'][2 + g][j * ROW_BLOCK:j * ROW_BLOCK + KEY_SPAN],
                           st['vmasks'])
        mix_scr[row_blocks[j], A_WIDTH + g * Q_GROUP_WIDTH:A_WIDTH + (g + 1) * Q_GROUP_WIDTH] = out

    def rms(j):
        rows = row_blocks[j]
        mi = jnp.concatenate([_rms_norm(mix_scr[rows, :A_WIDTH], nag_ref[...]),
                              _rms_norm(mix_scr[rows, A_WIDTH:], nbg_ref[...])], 1)
        mib_scr[rows, :] = mi.astype(BF16)

    def wout(nb):
        cols = slice(nb * WOUT_BLOCK, (nb + 1) * WOUT_BLOCK)
        m = jnp.dot(mib_scr[...], wout_ref[:, cols], preferred_element_type=F32)
        pre1_scr[:, cols] = ALPHA * x_ref[0, :, cols] + m

    def ln1(j):
        rows = row_blocks[j]
        x1 = _layer_norm(pre1_scr[rows, :], ln1g_ref[...], ln1b_ref[...])
        x1c_scr[rows, :] = x1
        x1b_scr[rows, :] = x1.astype(BF16)

    def ffn_dots(c):
        cs = slice(c * FF_CHUNK, (c + 1) * FF_CHUNK)
        x1b = x1b_scr[...]
        st['a', c] = jnp.dot(x1b, wg_ref[:, cs], preferred_element_type=F32)
        st['up', c] = jnp.dot(x1b, wu_ref[:, cs], preferred_element_type=F32)

    def ffn_epi(c):
        cs = slice(c * FF_CHUNK, (c + 1) * FF_CHUNK)
        a, up = st.pop(('a', c)), st.pop(('up', c))
        a1, a2 = _conv_taps(a, conv_scr[:, cs])
        cc = a2 * cw_ref[0:1, cs] + a1 * cw_ref[1:2, cs] + a * cw_ref[2:3, cs] + cb_ref[:, cs]
        h_scr[:, cs] = (_gelu(cc) * up).astype(BF16)
        conv_scr[:, cs] = a[tile - SUBLANES:]

    def down(nb):
        cols = slice(nb * DOWN_BLOCK, (nb + 1) * DOWN_BLOCK)
        f = jnp.dot(h_scr[...], wd_ref[:, cols], preferred_element_type=F32)
        st['pre2', nb] = ALPHA * x1c_scr[:, cols] + f

    def ln2():
        pre2 = jnp.concatenate([st.pop(('pre2', nb)) for nb in range(D_MODEL // DOWN_BLOCK)], 1)
        y_ref[0] = _layer_norm(pre2, ln2g_ref[...], ln2b_ref[...])

    assert n_proj == 7 and len(row_blocks) == 2 and len(attn_blocks) == 4 and N_FF_CHUNKS == 11
    proj_cast()
    proj_dot(0)
    ln1(0)
    proj_dot(1)
    ln1(1)
    proj_dot(2)
    ffn_dots(0)
    proj_dot(3)
    epi_u(0)
    ffn_dots(1)
    ffn_epi(0)
    proj_dot(4)
    epi_u(1)
    ffn_dots(2)
    ffn_epi(1)
    proj_dot(5)
    epi_gv()
    ffn_dots(3)
    ffn_epi(2)
    proj_dot(6)
    epi_q(4)
    ffn_dots(4)
    ffn_epi(3)
    gmlp()
    epi_q(5)
    ffn_dots(5)
    ffn_epi(4)
    epi_kv()
    attn_prep()
    for c in range(6, N_FF_CHUNKS):
        ffn_dots(c)
        ffn_epi(c - 1)
    attn_qk(0)
    attn_qk(1)
    ffn_epi(N_FF_CHUNKS - 1)
    attn_qk(2)
    attn_qk(3)
    down(0)
    attn_softmax(0)
    attn_softmax(1)
    down(1)
    attn_pv(0)
    attn_pv(1)
    attn_softmax(2)
    attn_softmax(3)
    down(2)
    attn_pv(2)
    attn_pv(3)
    rms(0)
    down(3)
    rms(1)
    wout(0)
    ln2()
    wout(1)
    pc_ref[0] = conv_scr[...]


def _sample_kernel(sinks_ref, x_ref, cos_ref, sin_ref, ck_ref, cv_ref, s1_ref, s2_ref,
                   win_ref, lng_ref, lnb_ref, wt_ref, bs_ref,
                   nag_ref, nbg_ref, wout_ref, ln1g_ref, ln1b_ref, wg_ref, wu_ref, cw_ref, cb_ref,
                   wd_ref, ln2g_ref, ln2b_ref,
                   y_ref, k_ref, v_ref, gv_ref, a_ref,
                   mix_scr, h_scr, *, n_batch, s_len):
    rows_all = n_batch * s_len
    x = x_ref[...]
    xb = x.astype(BF16)
    u, gv, q, k, v = _mixer_inputs(xb, win_ref, lng_ref[...], lnb_ref[...], cos_ref[...], sin_ref[...])
    gv_ref[...] = gv
    k_ref[...] = k
    v_ref[...] = v

    ri = _row_iota((rows_all, rows_all))
    ci = _lane_iota((rows_all, rows_all))
    same_batch = (ri // s_len) == (ci // s_len)
    causal = ((ri % s_len) // CHUNK) >= ((ci % s_len) // CHUNK)
    gvb = gv.astype(BF16)
    for h in range(A_HEADS):
        wm = jnp.where(same_batch & causal, wt_ref[h], 0.0).astype(BF16)
        cols = slice(h * A_HEAD_DIM, (h + 1) * A_HEAD_DIM)
        s = jnp.dot(wm, gvb[:, cols], preferred_element_type=F32) + bs_ref[h]
        mix_scr[:, cols] = u[:, cols] * s

    qb = q.astype(BF16)
    qmasks = _head_lane_masks(s_len)
    vmasks = _head_lane_masks(KEY_SPAN)
    allowed = _lane_iota((s_len, KEY_SPAN)) < WINDOW + s_len
    pad = jnp.zeros((KEY_SPAN - WINDOW - s_len, LANES), F32)
    for b in range(n_batch):
        rows = slice(b * s_len, (b + 1) * s_len)
        k_all = jnp.concatenate([ck_ref[b], k[rows], pad], 0)
        v_all = jnp.concatenate([cv_ref[b], v[rows], pad], 0)
        kreps = [a.astype(BF16) for a in _replicate_kv_heads(k_all)]
        vreps = [a.astype(BF16) for a in _replicate_kv_heads(v_all)]
        for g in range(B_KV_HEADS):
            sinks = [sinks_ref[g * GQA_GROUP + h] for h in range(GQA_GROUP)]
            qcols = slice(g * Q_GROUP_WIDTH, (g + 1) * Q_GROUP_WIDTH)
            out = _attn_block(qb[rows, qcols], kreps[g], vreps[g], allowed, sinks, qmasks, vmasks)
            mix_scr[rows, A_WIDTH + g * Q_GROUP_WIDTH:A_WIDTH + (g + 1) * Q_GROUP_WIDTH] = out

    x1 = _merge(x, mix_scr[...], nag_ref[...], nbg_ref[...], wout_ref, ln1g_ref[...], ln1b_ref[...])
    x1b = x1.astype(BF16)

    pos = _row_iota((rows_all, FF_CHUNK)) % s_len
    for c in range(N_FF_CHUNKS):
        cs = slice(c * FF_CHUNK, (c + 1) * FF_CHUNK)
        a = jnp.dot(x1b, wg_ref[:, cs], preferred_element_type=F32)
        up = jnp.dot(x1b, wu_ref[:, cs], preferred_element_type=F32)
        a_ref[:, cs] = a
        a1 = jnp.where(pos < 1, s1_ref[:, cs], pltpu.roll(a, 1, 0))
        a2 = jnp.where(pos < 2, s2_ref[:, cs], pltpu.roll(a, 2, 0))
        cc = a2 * cw_ref[0:1, cs] + a1 * cw_ref[1:2, cs] + a * cw_ref[2:3, cs] + cb_ref[:, cs]
        h_scr[:, cs] = (_gelu(cc) * up).astype(BF16)
    f = jnp.dot(h_scr[...], wd_ref[:, :D_MODEL], preferred_element_type=F32)
    y_ref[...] = _layer_norm(ALPHA * x1 + f, ln2g_ref[...], ln2b_ref[...])


def _rope_tables(pos):
    inv = ROPE_THETA ** (-np.arange(HALF, dtype=np.float64) / HALF)
    ang = pos.astype(np.float64)[:, None] * inv[None, :]
    cos, sin = np.cos(ang), np.sin(ang)
    reps = LANES // HEAD_DIM
    return (np.tile(np.concatenate([cos, cos], -1), (1, reps)).astype(np.float32),
            np.tile(np.concatenate([-sin, sin], -1), (1, reps)).astype(np.float32))


def _resident(shape):
    return pl.BlockSpec(shape, lambda *_: (0,) * len(shape), pipeline_mode=pl.Buffered(1))


def kernel(x_prompt, x_sample, cache_k, cache_v, state_ffn_conv, w_in, gmlp_ln_g, gmlp_ln_b,
           gmlp_w_s, gmlp_b_s, attn_sinks, norm_a_g, norm_b_g, w_out, ln1_g, ln1_b,
           w_gate, w_up, conv_w, conv_b, w_down, ln2_g, ln2_b):
    assert w_in.shape[0] == DEPTH == 1
    bp, tp, _ = x_prompt.shape
    bs, ts, _ = x_sample.shape
    tile = SEQ_TILE
    assert tp % tile == 0 and tile % ROW_BLOCK == 0 and tile >= WINDOW
    assert bs * ts == ROW_BLOCK and WINDOW + ts <= KEY_SPAN and ts >= CONV_WIDTH - 1

    row = lambda a: a[0].reshape(1, -1)
    pad_lanes = lambda w: jnp.pad(w, ((0, 0), (0, LANES)))
    win_b, wout_b = w_in[0].astype(BF16), pad_lanes(w_out[0].astype(BF16))
    wg_b, wu_b, wd_b = w_gate[0].astype(BF16), w_up[0].astype(BF16), pad_lanes(w_down[0].astype(BF16))
    sinks = attn_sinks[0]
    vec_args = dict(lng=row(gmlp_ln_g), lnb=row(gmlp_ln_b), nag=row(norm_a_g), nbg=row(norm_b_g),
                    ln1g=row(ln1_g), ln1b=row(ln1_b), cb=row(conv_b), ln2g=row(ln2_g), ln2b=row(ln2_b))
    cw = conv_w[0]
    smem = pl.BlockSpec(memory_space=pltpu.SMEM)

    def weight_specs(ws_shape):
        return [_resident(win_b.shape), _resident((1, A_WIDTH)), _resident((1, A_WIDTH)),
                _resident(ws_shape), _resident(ws_shape),
                _resident((1, A_WIDTH)), _resident((1, B_WIDTH)), _resident(wout_b.shape),
                _resident((1, D_MODEL)), _resident((1, D_MODEL)),
                _resident(wg_b.shape), _resident(wu_b.shape), _resident(cw.shape), _resident((1, D_FF)),
                _resident(wd_b.shape), _resident((1, D_MODEL)), _resident((1, D_MODEL))]

    def weight_args(ws, bsb):
        return (win_b, vec_args['lng'], vec_args['lnb'], ws, bsb, vec_args['nag'], vec_args['nbg'],
                wout_b, vec_args['ln1g'], vec_args['ln1b'], wg_b, wu_b, cw, vec_args['cb'], wd_b,
                vec_args['ln2g'], vec_args['ln2b'])

    cos_p, sin_p = _rope_tables(np.arange(tp))
    bias_p = jnp.broadcast_to(gmlp_b_s[0][:, :, None], (A_HEADS, GMLP_CHUNK, A_HEAD_DIM))
    seq_tiles = tp // tile
    n_tiles = bp * seq_tiles
    def stage_bt(lag):
        def bt(s):
            i = jnp.clip(s - lag, 0, n_tiles - 1)
            return i // seq_tiles, i % seq_tiles
        return bt
    mixer_bt, ffn_bt = stage_bt(0), stage_bt(1)
    y_p, pk, pv, pc = pl.pallas_call(
        functools.partial(_prompt_kernel, tile=tile, seq_tiles=seq_tiles, n_tiles=n_tiles),
        grid=(n_tiles + 1,),
        in_specs=[smem,
                  pl.BlockSpec((1, tile, D_MODEL), lambda s: (*mixer_bt(s), 0)),
                  _resident(cos_p.shape), _resident(sin_p.shape)]
                 + weight_specs((A_HEADS, GMLP_CHUNK, GMLP_CHUNK)),
        out_specs=[pl.BlockSpec((1, tile, D_MODEL), lambda s: (*ffn_bt(s), 0)),
                   pl.BlockSpec((1, WINDOW, KV_WIDTH), lambda s: (mixer_bt(s)[0], 0, 0)),
                   pl.BlockSpec((1, WINDOW, KV_WIDTH), lambda s: (mixer_bt(s)[0], 0, 0)),
                   pl.BlockSpec((1, SUBLANES, D_FF), lambda s: (ffn_bt(s)[0], 0, 0))],
        out_shape=[jax.ShapeDtypeStruct((bp, tp, D_MODEL), F32),
                   jax.ShapeDtypeStruct((bp, WINDOW, KV_WIDTH), F32),
                   jax.ShapeDtypeStruct((bp, WINDOW, KV_WIDTH), F32),
                   jax.ShapeDtypeStruct((bp, SUBLANES, D_FF), F32)],
        scratch_shapes=[pltpu.VMEM((4, WINDOW, LANES), BF16),
                        pltpu.VMEM((SUBLANES, D_FF), F32),
                        pltpu.VMEM((tile, MIX_WIDTH), F32),
                        pltpu.VMEM((tile, D_FF), BF16),
                        pltpu.VMEM((tile, D_MODEL), F32),
                        pltpu.VMEM((tile, D_MODEL), F32),
                        pltpu.VMEM((tile, D_MODEL), BF16),
                        pltpu.VMEM((tile, D_MODEL), BF16),
                        pltpu.VMEM((tile, MIX_WIDTH), BF16)],
        compiler_params=pltpu.CompilerParams(dimension_semantics=("arbitrary",),
                                             vmem_limit_bytes=V7X_VMEM_LIMIT_BYTES),
        name="prompt_layer",
    )(sinks, x_prompt, cos_p, sin_p, *weight_args(gmlp_w_s[0], bias_p))

    rows_all = bs * ts
    cos_s, sin_s = _rope_tables(PAST_LEN + np.arange(ts))
    cos_s, sin_s = np.tile(cos_s, (bs, 1)), np.tile(sin_s, (bs, 1))
    ck = cache_k[0].reshape(bs, WINDOW, KV_WIDTH)
    cv = cache_v[0].reshape(bs, WINDOW, KV_WIDTH)
    st = state_ffn_conv[0]
    pad_frames = lambda a: jnp.pad(a, ((0, 0), (0, ts - a.shape[1]), (0, 0))).reshape(rows_all, D_FF)
    s1, s2 = pad_frames(st[:, 1:]), pad_frames(st)
    w_tiled = jnp.tile(gmlp_w_s[0][:, :ts, :ts], (1, bs, bs))
    bias_s = jnp.broadcast_to(jnp.tile(gmlp_b_s[0][:, :ts], (1, bs))[:, :, None],
                              (A_HEADS, rows_all, A_HEAD_DIM))
    full = lambda shape: pl.BlockSpec(shape, lambda i: (0,) * len(shape))
    y_s, k_s, v_s, gv_s, a_s = pl.pallas_call(
        functools.partial(_sample_kernel, n_batch=bs, s_len=ts),
        grid=(1,),
        in_specs=[smem, _resident((rows_all, D_MODEL)), _resident(cos_s.shape), _resident(sin_s.shape),
                  _resident(ck.shape), _resident(cv.shape), _resident(s1.shape), _resident(s2.shape)]
                 + weight_specs((A_HEADS, rows_all, rows_all)),
        out_specs=[full((rows_all, D_MODEL)), full((rows_all, KV_WIDTH)), full((rows_all, KV_WIDTH)),
                   full((rows_all, A_WIDTH)), full((rows_all, D_FF))],
        out_shape=[jax.ShapeDtypeStruct((rows_all, D_MODEL), F32),
                   jax.ShapeDtypeStruct((rows_all, KV_WIDTH), F32),
                   jax.ShapeDtypeStruct((rows_all, KV_WIDTH), F32),
                   jax.ShapeDtypeStruct((rows_all, A_WIDTH), F32),
                   jax.ShapeDtypeStruct((rows_all, D_FF), F32)],
        scratch_shapes=[pltpu.VMEM((rows_all, MIX_WIDTH), F32),
                        pltpu.VMEM((rows_all, D_FF), BF16)],
        compiler_params=pltpu.CompilerParams(dimension_semantics=("arbitrary",),
                                             vmem_limit_bytes=V7X_VMEM_LIMIT_BYTES),
        name="sample_layer",
    )(sinks, x_sample.reshape(rows_all, D_MODEL), cos_s, sin_s, ck, cv, s1, s2,
      *weight_args(w_tiled, bias_s))

    kv5 = lambda a, n, t: a.reshape(1, n, t, B_KV_HEADS, HEAD_DIM)
    return (y_p, y_s.reshape(bs, ts, D_MODEL),
            kv5(pk, bp, WINDOW), kv5(pv, bp, WINDOW),
            pc[None, :, SUBLANES - (CONV_WIDTH - 1):],
            kv5(k_s, bs, ts), kv5(v_s, bs, ts),
            gv_s.reshape(1, bs, ts, A_HEADS, A_HEAD_DIM),
            a_s.reshape(bs, ts, D_FF)[None, :, ts - (CONV_WIDTH - 1):])
```

```python
import functools
import math

import jax
import jax.numpy as jnp
import numpy as np
from jax import lax
from jax.experimental import pallas as pl
from jax.experimental.pallas import tpu as pltpu

D_MODEL = 1024
CHUNK = 64
HEAD_DIM = 64
HALF = HEAD_DIM // 2
A_HEADS = 4
A_HEAD_DIM = 128
A_WIDTH = A_HEADS * A_HEAD_DIM
GMLP_CHUNK = 128
B_HEADS = 8
B_KV_HEADS = 2
GQA_GROUP = B_HEADS // B_KV_HEADS
B_WIDTH = B_HEADS * HEAD_DIM
KV_WIDTH = B_KV_HEADS * HEAD_DIM
WINDOW = 128
MIX_WIDTH = A_WIDTH + B_WIDTH
D_FF = 2816
CONV_WIDTH = 3
PAST_LEN = 2048
ROPE_THETA = 10000.0
LN_EPS = 1e-5
RMS_EPS = 1e-6
DEPTH = 1
ALPHA = (2 * DEPTH) ** 0.25
ATTN_SCALE = HEAD_DIM ** -0.5
LOG2_E = math.log2(math.e)
QK_SCALE = ATTN_SCALE * LOG2_E

LANES = 128
SUBLANES = 8
V7X_VMEM_LIMIT_BYTES = 56 * 1024 * 1024

Q_GROUP_WIDTH = GQA_GROUP * HEAD_DIM
KEY_SPAN = 2 * WINDOW
ROW_BLOCK = 128
FF_CHUNK = 256
N_FF_CHUNKS = D_FF // FF_CHUNK
PROJ_BLOCK = 256
DOWN_BLOCK = 256
WOUT_BLOCK = 512
SEQ_TILE = 256
TILES_PER_STEP = 2

F32 = jnp.float32
BF16 = jnp.bfloat16


def _gelu(x):
    k0 = -2.0 * math.sqrt(2.0 / math.pi) * math.log2(math.e)
    k1 = k0 * 0.044715
    return x * (1.0 / (1.0 + jnp.exp2(x * (k0 + k1 * (x * x)))))


def _layer_norm(x, g, b):
    mu = jnp.mean(x, -1, keepdims=True)
    xc = x - mu
    var = jnp.mean(xc * xc, -1, keepdims=True)
    return xc * lax.rsqrt(var + LN_EPS) * g + b


def _rms_norm(x, g):
    ms = jnp.mean(x * x, -1, keepdims=True)
    return x * lax.rsqrt(ms + RMS_EPS) * g


def _lane_iota(shape):
    return lax.broadcasted_iota(jnp.int32, shape, 1)


def _row_iota(shape):
    return lax.broadcasted_iota(jnp.int32, shape, 0)


def _rope(x, cos, sin_signed):
    lo = (_lane_iota(cos.shape) % HEAD_DIM) < HALF
    outs = []
    for i in range(x.shape[1] // LANES):
        xs = x[:, i * LANES:(i + 1) * LANES]
        partner = jnp.where(lo, pltpu.roll(xs, LANES - HALF, 1), pltpu.roll(xs, HALF, 1))
        outs.append(xs * cos + partner * sin_signed)
    return outs[0] if len(outs) == 1 else jnp.concatenate(outs, 1)


def _replicate_kv_heads(x):
    lo = _lane_iota(x.shape) < HEAD_DIM
    sw = pltpu.roll(x, HEAD_DIM, 1)
    return jnp.where(lo, x, sw), jnp.where(lo, sw, x)


def _head_lane_masks(rows):
    lane = _lane_iota((rows, Q_GROUP_WIDTH))
    return [jnp.where((lane >= h * HEAD_DIM) & (lane < (h + 1) * HEAD_DIM), 1.0, 0.0).astype(BF16)
            for h in range(GQA_GROUP)]


def _attn_scores(qg, krep, qmasks):
    k2 = jnp.concatenate([krep, krep], 1)
    qm = jnp.concatenate([qg * qmasks[h] for h in range(GQA_GROUP)], 0)
    return lax.dot_general(qm, k2, (((1,), (1,)), ((), ())), preferred_element_type=F32)


def _attn_probs(sc, allowed, sinks):
    r = sc.shape[0] // GQA_GROUP
    ps = []
    for h in range(GQA_GROUP):
        s = jnp.where(allowed, sc[h * r:(h + 1) * r], -jnp.inf)
        sink = sinks[h] * LOG2_E
        m = jnp.maximum(jnp.max(s, -1, keepdims=True), sink)
        e = jnp.exp2(s - m)
        den = jnp.sum(e, -1, keepdims=True) + jnp.exp2(sink - m)
        ps.append((e * (1.0 / den)).astype(BF16))
    return jnp.concatenate(ps, 1)


def _attn_values(pc, vrep, vmasks):
    v2 = jnp.concatenate([vrep, vrep], 1)
    vm = jnp.concatenate([v2 * vmasks[h] for h in range(GQA_GROUP)], 0)
    return jnp.dot(pc, vm, preferred_element_type=F32)


def _attn_block(qg, krep, vrep, allowed, sinks, qmasks, vmasks):
    return _attn_values(_attn_probs(_attn_scores(qg, krep, qmasks), allowed, sinks), vrep, vmasks)


def _mixer_inputs(xb, win_ref, lng, lnb, cos, sin_signed):
    za = jnp.dot(xb, win_ref[:, :2 * A_WIDTH], preferred_element_type=F32)
    u = _gelu(za[:, :A_WIDTH])
    gv = _layer_norm(_gelu(za[:, A_WIDTH:]), lng, lnb)
    zb = jnp.dot(xb, win_ref[:, 2 * A_WIDTH:], preferred_element_type=F32)
    qs = _rope(zb[:, :B_WIDTH], cos * QK_SCALE, sin_signed * QK_SCALE)
    k = _rope(zb[:, B_WIDTH:B_WIDTH + KV_WIDTH], cos, sin_signed)
    v = zb[:, B_WIDTH + KV_WIDTH:]
    return u, gv, qs, k, v


def _merge(x, mix, nag, nbg, wout_ref, ln1g, ln1b):
    mi = jnp.concatenate([_rms_norm(mix[:, :A_WIDTH], nag), _rms_norm(mix[:, A_WIDTH:], nbg)], 1)
    m = jnp.dot(mi.astype(BF16), wout_ref[:, :D_MODEL], preferred_element_type=F32)
    return _layer_norm(ALPHA * x + m, ln1g, ln1b)


def _conv_taps(a, prev8):
    r1 = pltpu.roll(a, 1, 0)
    r2 = pltpu.roll(a, 2, 0)
    row = _row_iota(prev8.shape)
    first1 = jnp.where(row < 1, pltpu.roll(prev8, 1, 0), r1[:SUBLANES])
    first2 = jnp.where(row < 2, pltpu.roll(prev8, 2, 0), r2[:SUBLANES])
    return (jnp.concatenate([first1, r1[SUBLANES:]], 0), jnp.concatenate([first2, r2[SUBLANES:]], 0))


def _prompt_kernel(sinks_ref, x_ref, cos_ref, sin_ref, win_ref, lng_ref, lnb_ref, ws_ref, bs_ref,
                   nag_ref, nbg_ref, wout_ref, ln1g_ref, ln1b_ref, wg_ref, wu_ref, cw_ref, cb_ref,
                   wd_ref, ln2g_ref, ln2b_ref,
                   y_ref, pk_ref, pv_ref, pc_ref,
                   kv_scr, conv_scr, mix_scr, h_scr, pre1_scr, x1c_scr, x1b_scr, xb_scr, mib_scr,
                   *, tile, seq_tiles, n_steps):
    s = pl.program_id(0)
    mixer_tile0 = TILES_PER_STEP * jnp.minimum(s, n_steps - 2)
    ffn_tile0 = TILES_PER_STEP * jnp.maximum(s - 1, 0)

    @pl.when(s == 0)
    def _():
        pre1_scr[...] = jnp.zeros_like(pre1_scr)

    @pl.when(mixer_tile0 % seq_tiles == 0)
    def _():
        kv_scr[...] = jnp.zeros_like(kv_scr)

    @pl.when(ffn_tile0 % seq_tiles == 0)
    def _():
        conv_scr[...] = jnp.zeros_like(conv_scr)

    row_blocks = [slice(j * ROW_BLOCK, (j + 1) * ROW_BLOCK) for j in range(tile // ROW_BLOCK)]
    attn_blocks = [(j, g) for j in range(tile // ROW_BLOCK) for g in range(B_KV_HEADS)]
    n_proj = win_ref.shape[1] // PROJ_BLOCK
    assert n_proj == 7 and len(row_blocks) == 2 and len(attn_blocks) == 4 and N_FF_CHUNKS == 11

    def pipeline_slot(k):
        tm = (mixer_tile0 + k) % seq_tiles
        trows = slice(k * tile, (k + 1) * tile)
        pre1 = pre1_scr.at[k]
        st = {}

        def proj_cast():
            xb_scr[...] = x_ref[0, trows, :].astype(BF16)

        def proj_dot(i):
            cols = slice(i * PROJ_BLOCK, (i + 1) * PROJ_BLOCK)
            st['z', i] = jnp.dot(xb_scr[...], win_ref[:, cols], preferred_element_type=F32)

        def epi_u(i):
            st['u', i] = _gelu(st.pop(('z', i)))

        def epi_gv():
            g = jnp.concatenate([_gelu(st.pop(('z', 2))), _gelu(st.pop(('z', 3)))], 1)
            st['gvb'] = _layer_norm(g, lng_ref[...], lnb_ref[...]).astype(BF16)

        def rope_tables():
            row0 = pl.multiple_of(tm * tile, tile)
            return cos_ref[pl.ds(row0, tile), :], sin_ref[pl.ds(row0, tile), :]

        def epi_q(i):
            cos, sin_signed = rope_tables()
            st['qb', i - 4] = _rope(st.pop(('z', i)), cos * QK_SCALE, sin_signed * QK_SCALE).astype(BF16)

        def epi_kv():
            cos, sin_signed = rope_tables()
            zkv = st.pop(('z', 6))
            kk = _rope(zkv[:, :KV_WIDTH], cos, sin_signed)
            vv = zkv[:, KV_WIDTH:]
            pk_ref[0] = kk[tile - WINDOW:]
            pv_ref[0] = vv[tile - WINDOW:]
            k0, k1 = _replicate_kv_heads(kk)
            v0, v1 = _replicate_kv_heads(vv)
            reps = [jnp.concatenate([kv_scr[i], a.astype(BF16)], 0) for i, a in enumerate((k0, k1, v0, v1))]
            for i in range(4):
                kv_scr[i] = reps[i][tile:]
            st['reps'] = reps

        def gmlp():
            cidx_r = _row_iota((GMLP_CHUNK, GMLP_CHUNK)) // CHUNK
            cidx_c = _lane_iota((GMLP_CHUNK, GMLP_CHUNK)) // CHUNK
            gvb = st.pop('gvb')
            zero = jnp.zeros((GMLP_CHUNK, A_HEAD_DIM), BF16)
            for hp in range(A_HEADS // 2):
                ha, hb = 2 * hp, 2 * hp + 1
                wm = jnp.concatenate([jnp.where(cidx_r >= cidx_c, ws_ref[h], 0.0) for h in (ha, hb)],
                                     1).astype(BF16)
                bias = jnp.concatenate([bs_ref[ha], bs_ref[hb]], 1)
                cols = slice(ha * A_HEAD_DIM, (hb + 1) * A_HEAD_DIM)
                u = st.pop(('u', hp))
                for c in range(tile // GMLP_CHUNK):
                    rows = slice(c * GMLP_CHUNK, (c + 1) * GMLP_CHUNK)
                    ga = gvb[rows, ha * A_HEAD_DIM:(ha + 1) * A_HEAD_DIM]
                    gb = gvb[rows, hb * A_HEAD_DIM:(hb + 1) * A_HEAD_DIM]
                    rhs = jnp.concatenate([jnp.concatenate([ga, zero], 1), jnp.concatenate([zero, gb], 1)], 0)
                    sg = jnp.dot(wm, rhs, preferred_element_type=F32) + bias
                    mix_scr[rows, cols] = u[rows, :] * sg

        def attn_prep():
            qc = _row_iota((ROW_BLOCK, KEY_SPAN)) // CHUNK
            kc = _lane_iota((ROW_BLOCK, KEY_SPAN)) // CHUNK
            in_band = (kc >= qc) & (kc <= qc + WINDOW // CHUNK)
            first_lo = jnp.where(tm > 0, 0, WINDOW // CHUNK)
            st.update(in_band=in_band, allowed_first=in_band & (kc >= first_lo),
                      qmasks=_head_lane_masks(ROW_BLOCK), vmasks=_head_lane_masks(KEY_SPAN))

        def attn_qk(i):
            j, g = attn_blocks[i]
            qg = st['qb', g][row_blocks[j], :]
            st['sc', i] = _attn_scores(qg, st['reps'][g][j * ROW_BLOCK:j * ROW_BLOCK + KEY_SPAN],
                                       st['qmasks'])

        def attn_softmax(i):
            j, g = attn_blocks[i]
            sinks = [sinks_ref[g * GQA_GROUP + h] for h in range(GQA_GROUP)]
            allowed = st['allowed_first'] if j == 0 else st['in_band']
            st['p', i] = _attn_probs(st.pop(('sc', i)), allowed, sinks)

        def attn_pv(i):
            j, g = attn_blocks[i]
            out = _attn_values(st.pop(('p', i)),
                               st['reps'][2 + g][j * ROW_BLOCK:j * ROW_BLOCK + KEY_SPAN], st['vmasks'])
            mix_scr[row_blocks[j], A_WIDTH + g * Q_GROUP_WIDTH:A_WIDTH + (g + 1) * Q_GROUP_WIDTH] = out

        def rms(j):
            rows = row_blocks[j]
            mi = jnp.concatenate([_rms_norm(mix_scr[rows, :A_WIDTH], nag_ref[...]),
                                  _rms_norm(mix_scr[rows, A_WIDTH:], nbg_ref[...])], 1)
            mib_scr[rows, :] = mi.astype(BF16)

        def wout(nb):
            cols = slice(nb * WOUT_BLOCK, (nb + 1) * WOUT_BLOCK)
            m = jnp.dot(mib_scr[...], wout_ref[:, cols], preferred_element_type=F32)
            pre1[:, cols] = ALPHA * x_ref[0, trows, cols] + m

        def ln1(j):
            rows = row_blocks[j]
            x1 = _layer_norm(pre1[rows, :], ln1g_ref[...], ln1b_ref[...])
            x1c_scr[rows, :] = x1
            x1b_scr[rows, :] = x1.astype(BF16)

        def ffn_dots(c):
            cs = slice(c * FF_CHUNK, (c + 1) * FF_CHUNK)
            x1b = x1b_scr[...]
            st['a', c] = jnp.dot(x1b, wg_ref[:, cs], preferred_element_type=F32)
            st['up', c] = jnp.dot(x1b, wu_ref[:, cs], preferred_element_type=F32)

        def ffn_epi(c):
            cs = slice(c * FF_CHUNK, (c + 1) * FF_CHUNK)
            a, up = st.pop(('a', c)), st.pop(('up', c))
            a1, a2 = _conv_taps(a, conv_scr[:, cs])
            cc = a2 * cw_ref[0:1, cs] + a1 * cw_ref[1:2, cs] + a * cw_ref[2:3, cs] + cb_ref[:, cs]
            h_scr[:, cs] = (_gelu(cc) * up).astype(BF16)
            conv_scr[:, cs] = a[tile - SUBLANES:]

        def down(nb):
            cols = slice(nb * DOWN_BLOCK, (nb + 1) * DOWN_BLOCK)
            f = jnp.dot(h_scr[...], wd_ref[:, cols], preferred_element_type=F32)
            st['pre2', nb] = ALPHA * x1c_scr[:, cols] + f

        def ln2():
            pre2 = jnp.concatenate([st.pop(('pre2', nb)) for nb in range(D_MODEL // DOWN_BLOCK)], 1)
            y_ref[0, trows, :] = _layer_norm(pre2, ln2g_ref[...], ln2b_ref[...])

        proj_cast()
        proj_dot(0)
        ln1(0)
        proj_dot(1)
        ln1(1)
        proj_dot(2)
        ffn_dots(0)
        proj_dot(3)
        epi_u(0)
        ffn_dots(1)
        ffn_epi(0)
        proj_dot(4)
        epi_u(1)
        ffn_dots(2)
        ffn_epi(1)
        proj_dot(5)
        epi_gv()
        ffn_dots(3)
        ffn_epi(2)
        proj_dot(6)
        epi_q(4)
        ffn_dots(4)
        ffn_epi(3)
        gmlp()
        epi_q(5)
        ffn_dots(5)
        ffn_epi(4)
        epi_kv()
        attn_prep()
        for c in range(6, N_FF_CHUNKS):
            ffn_dots(c)
            ffn_epi(c - 1)
        attn_qk(0)
        attn_qk(1)
        ffn_epi(N_FF_CHUNKS - 1)
        attn_qk(2)
        attn_qk(3)
        down(0)
        attn_softmax(0)
        attn_softmax(1)
        down(1)
        attn_pv(0)
        attn_pv(1)
        attn_softmax(2)
        attn_softmax(3)
        down(2)
        attn_pv(2)
        attn_pv(3)
        rms(0)
        down(3)
        rms(1)
        wout(0)
        ln2()
        wout(1)

    for k in range(TILES_PER_STEP):
        pipeline_slot(k)
    pc_ref[0] = conv_scr[...]


def _sample_kernel(sinks_ref, x_ref, cos_ref, sin_ref, ck_ref, cv_ref, s1_ref, s2_ref,
                   win_ref, lng_ref, lnb_ref, wt_ref, bs_ref,
                   nag_ref, nbg_ref, wout_ref, ln1g_ref, ln1b_ref, wg_ref, wu_ref, cw_ref, cb_ref,
                   wd_ref, ln2g_ref, ln2b_ref,
                   y_ref, k_ref, v_ref, gv_ref, a_ref,
                   mix_scr, h_scr, *, n_batch, s_len):
    rows_all = n_batch * s_len
    x = x_ref[...]
    xb = x.astype(BF16)
    u, gv, q, k, v = _mixer_inputs(xb, win_ref, lng_ref[...], lnb_ref[...], cos_ref[...], sin_ref[...])
    gv_ref[...] = gv
    k_ref[...] = k
    v_ref[...] = v

    ri = _row_iota((rows_all, rows_all))
    ci = _lane_iota((rows_all, rows_all))
    same_batch = (ri // s_len) == (ci // s_len)
    causal = ((ri % s_len) // CHUNK) >= ((ci % s_len) // CHUNK)
    gvb = gv.astype(BF16)
    for h in range(A_HEADS):
        wm = jnp.where(same_batch & causal, wt_ref[h], 0.0).astype(BF16)
        cols = slice(h * A_HEAD_DIM, (h + 1) * A_HEAD_DIM)
        s = jnp.dot(wm, gvb[:, cols], preferred_element_type=F32) + bs_ref[h]
        mix_scr[:, cols] = u[:, cols] * s

    qb = q.astype(BF16)
    qmasks = _head_lane_masks(s_len)
    vmasks = _head_lane_masks(KEY_SPAN)
    allowed = _lane_iota((s_len, KEY_SPAN)) < WINDOW + s_len
    pad = jnp.zeros((KEY_SPAN - WINDOW - s_len, LANES), F32)
    for b in range(n_batch):
        rows = slice(b * s_len, (b + 1) * s_len)
        k_all = jnp.concatenate([ck_ref[b], k[rows], pad], 0)
        v_all = jnp.concatenate([cv_ref[b], v[rows], pad], 0)
        kreps = [a.astype(BF16) for a in _replicate_kv_heads(k_all)]
        vreps = [a.astype(BF16) for a in _replicate_kv_heads(v_all)]
        for g in range(B_KV_HEADS):
            sinks = [sinks_ref[g * GQA_GROUP + h] for h in range(GQA_GROUP)]
            qcols = slice(g * Q_GROUP_WIDTH, (g + 1) * Q_GROUP_WIDTH)
            out = _attn_block(qb[rows, qcols], kreps[g], vreps[g], allowed, sinks, qmasks, vmasks)
            mix_scr[rows, A_WIDTH + g * Q_GROUP_WIDTH:A_WIDTH + (g + 1) * Q_GROUP_WIDTH] = out

    x1 = _merge(x, mix_scr[...], nag_ref[...], nbg_ref[...], wout_ref, ln1g_ref[...], ln1b_ref[...])
    x1b = x1.astype(BF16)

    pos = _row_iota((rows_all, FF_CHUNK)) % s_len
    for c in range(N_FF_CHUNKS):
        cs = slice(c * FF_CHUNK, (c + 1) * FF_CHUNK)
        a = jnp.dot(x1b, wg_ref[:, cs], preferred_element_type=F32)
        up = jnp.dot(x1b, wu_ref[:, cs], preferred_element_type=F32)
        a_ref[:, cs] = a
        a1 = jnp.where(pos < 1, s1_ref[:, cs], pltpu.roll(a, 1, 0))
        a2 = jnp.where(pos < 2, s2_ref[:, cs], pltpu.roll(a, 2, 0))
        cc = a2 * cw_ref[0:1, cs] + a1 * cw_ref[1:2, cs] + a * cw_ref[2:3, cs] + cb_ref[:, cs]
        h_scr[:, cs] = (_gelu(cc) * up).astype(BF16)
    f = jnp.dot(h_scr[...], wd_ref[:, :D_MODEL], preferred_element_type=F32)
    y_ref[...] = _layer_norm(ALPHA * x1 + f, ln2g_ref[...], ln2b_ref[...])


def _rope_tables(pos):
    inv = ROPE_THETA ** (-np.arange(HALF, dtype=np.float64) / HALF)
    ang = pos.astype(np.float64)[:, None] * inv[None, :]
    cos, sin = np.cos(ang), np.sin(ang)
    reps = LANES // HEAD_DIM
    return (np.tile(np.concatenate([cos, cos], -1), (1, reps)).astype(np.float32),
            np.tile(np.concatenate([-sin, sin], -1), (1, reps)).astype(np.float32))


def _resident(shape):
    return pl.BlockSpec(shape, lambda *_: (0,) * len(shape), pipeline_mode=pl.Buffered(1))


def kernel(x_prompt, x_sample, cache_k, cache_v, state_ffn_conv, w_in, gmlp_ln_g, gmlp_ln_b,
           gmlp_w_s, gmlp_b_s, attn_sinks, norm_a_g, norm_b_g, w_out, ln1_g, ln1_b,
           w_gate, w_up, conv_w, conv_b, w_down, ln2_g, ln2_b):
    assert w_in.shape[0] == DEPTH == 1
    bp, tp, _ = x_prompt.shape
    bs, ts, _ = x_sample.shape
    tile = SEQ_TILE
    step_rows = TILES_PER_STEP * tile
    assert tp % step_rows == 0 and tile % ROW_BLOCK == 0 and tile >= WINDOW
    assert bs * ts == ROW_BLOCK and WINDOW + ts <= KEY_SPAN and ts >= CONV_WIDTH - 1

    row = lambda a: a[0].reshape(1, -1)
    pad_lanes = lambda w: jnp.pad(w, ((0, 0), (0, LANES)))
    win_b, wout_b = w_in[0].astype(BF16), pad_lanes(w_out[0].astype(BF16))
    wg_b, wu_b, wd_b = w_gate[0].astype(BF16), w_up[0].astype(BF16), pad_lanes(w_down[0].astype(BF16))
    sinks = attn_sinks[0]
    vec_args = dict(lng=row(gmlp_ln_g), lnb=row(gmlp_ln_b), nag=row(norm_a_g), nbg=row(norm_b_g),
                    ln1g=row(ln1_g), ln1b=row(ln1_b), cb=row(conv_b), ln2g=row(ln2_g), ln2b=row(ln2_b))
    cw = conv_w[0]
    smem = pl.BlockSpec(memory_space=pltpu.SMEM)

    def weight_specs(ws_shape):
        return [_resident(win_b.shape), _resident((1, A_WIDTH)), _resident((1, A_WIDTH)),
                _resident(ws_shape), _resident(ws_shape),
                _resident((1, A_WIDTH)), _resident((1, B_WIDTH)), _resident(wout_b.shape),
                _resident((1, D_MODEL)), _resident((1, D_MODEL)),
                _resident(wg_b.shape), _resident(wu_b.shape), _resident(cw.shape), _resident((1, D_FF)),
                _resident(wd_b.shape), _resident((1, D_MODEL)), _resident((1, D_MODEL))]

    def weight_args(ws, bsb):
        return (win_b, vec_args['lng'], vec_args['lnb'], ws, bsb, vec_args['nag'], vec_args['nbg'],
                wout_b, vec_args['ln1g'], vec_args['ln1b'], wg_b, wu_b, cw, vec_args['cb'], wd_b,
                vec_args['ln2g'], vec_args['ln2b'])

    cos_p, sin_p = _rope_tables(np.arange(tp))
    bias_p = jnp.broadcast_to(gmlp_b_s[0][:, :, None], (A_HEADS, GMLP_CHUNK, A_HEAD_DIM))
    seq_tiles = tp // tile
    seq_blocks = tp // step_rows
    n_blocks = bp * seq_blocks
    n_steps = n_blocks + 1
    def stage_bt(lag):
        def bt(s):
            i = jnp.clip(s - lag, 0, n_blocks - 1)
            return i // seq_blocks, i % seq_blocks
        return bt
    mixer_bt, ffn_bt = stage_bt(0), stage_bt(1)
    y_p, pk, pv, pc = pl.pallas_call(
        functools.partial(_prompt_kernel, tile=tile, seq_tiles=seq_tiles, n_steps=n_steps),
        grid=(n_steps,),
        in_specs=[smem,
                  pl.BlockSpec((1, step_rows, D_MODEL), lambda s: (*mixer_bt(s), 0)),
                  _resident(cos_p.shape), _resident(sin_p.shape)]
                 + weight_specs((A_HEADS, GMLP_CHUNK, GMLP_CHUNK)),
        out_specs=[pl.BlockSpec((1, step_rows, D_MODEL), lambda s: (*ffn_bt(s), 0)),
                   pl.BlockSpec((1, WINDOW, KV_WIDTH), lambda s: (mixer_bt(s)[0], 0, 0)),
                   pl.BlockSpec((1, WINDOW, KV_WIDTH), lambda s: (mixer_bt(s)[0], 0, 0)),
                   pl.BlockSpec((1, SUBLANES, D_FF), lambda s: (ffn_bt(s)[0], 0, 0))],
        out_shape=[jax.ShapeDtypeStruct((bp, tp, D_MODEL), F32),
                   jax.ShapeDtypeStruct((bp, WINDOW, KV_WIDTH), F32),
                   jax.ShapeDtypeStruct((bp, WINDOW, KV_WIDTH), F32),
                   jax.ShapeDtypeStruct((bp, SUBLANES, D_FF), F32)],
        scratch_shapes=[pltpu.VMEM((4, WINDOW, LANES), BF16),
                        pltpu.VMEM((SUBLANES, D_FF), F32),
                        pltpu.VMEM((tile, MIX_WIDTH), F32),
                        pltpu.VMEM((tile, D_FF), BF16),
                        pltpu.VMEM((TILES_PER_STEP, tile, D_MODEL), F32),
                        pltpu.VMEM((tile, D_MODEL), F32),
                        pltpu.VMEM((tile, D_MODEL), BF16),
                        pltpu.VMEM((tile, D_MODEL), BF16),
                        pltpu.VMEM((tile, MIX_WIDTH), BF16)],
        compiler_params=pltpu.CompilerParams(dimension_semantics=("arbitrary",),
                                             vmem_limit_bytes=V7X_VMEM_LIMIT_BYTES),
        name="prompt_layer",
    )(sinks, x_prompt, cos_p, sin_p, *weight_args(gmlp_w_s[0], bias_p))

    rows_all = bs * ts
    cos_s, sin_s = _rope_tables(PAST_LEN + np.arange(ts))
    cos_s, sin_s = np.tile(cos_s, (bs, 1)), np.tile(sin_s, (bs, 1))
    ck = cache_k[0].reshape(bs, WINDOW, KV_WIDTH)
    cv = cache_v[0].reshape(bs, WINDOW, KV_WIDTH)
    st = state_ffn_conv[0]
    pad_frames = lambda a: jnp.pad(a, ((0, 0), (0, ts - a.shape[1]), (0, 0))).reshape(rows_all, D_FF)
    s1, s2 = pad_frames(st[:, 1:]), pad_frames(st)
    w_tiled = jnp.tile(gmlp_w_s[0][:, :ts, :ts], (1, bs, bs))
    bias_s = jnp.broadcast_to(jnp.tile(gmlp_b_s[0][:, :ts], (1, bs))[:, :, None],
                              (A_HEADS, rows_all, A_HEAD_DIM))
    full = lambda shape: pl.BlockSpec(shape, lambda i: (0,) * len(shape))
    y_s, k_s, v_s, gv_s, a_s = pl.pallas_call(
        functools.partial(_sample_kernel, n_batch=bs, s_len=ts),
        grid=(1,),
        in_specs=[smem, _resident((rows_all, D_MODEL)), _resident(cos_s.shape), _resident(sin_s.shape),
                  _resident(ck.shape), _resident(cv.shape), _resident(s1.shape), _resident(s2.shape)]
                 + weight_specs((A_HEADS, rows_all, rows_all)),
        out_specs=[full((rows_all, D_MODEL)), full((rows_all, KV_WIDTH)), full((rows_all, KV_WIDTH)),
                   full((rows_all, A_WIDTH)), full((rows_all, D_FF))],
        out_shape=[jax.ShapeDtypeStruct((rows_all, D_MODEL), F32),
                   jax.ShapeDtypeStruct((rows_all, KV_WIDTH), F32),
                   jax.ShapeDtypeStruct((rows_all, KV_WIDTH), F32),
                   jax.ShapeDtypeStruct((rows_all, A_WIDTH), F32),
                   jax.ShapeDtypeStruct((rows_all, D_FF), F32)],
        scratch_shapes=[pltpu.VMEM((rows_all, MIX_WIDTH), F32),
                        pltpu.VMEM((rows_all, D_FF), BF16)],
        compiler_params=pltpu.CompilerParams(dimension_semantics=("arbitrary",),
                                             vmem_limit_bytes=V7X_VMEM_LIMIT_BYTES),
        name="sample_layer",
    )(sinks, x_sample.reshape(rows_all, D_MODEL), cos_s, sin_s, ck, cv, s1, s2,
      *weight_args(w_tiled, bias_s))

    kv5 = lambda a, n, t: a.reshape(1, n, t, B_KV_HEADS, HEAD_DIM)
    return (y_p, y_s.reshape(bs, ts, D_MODEL),
            kv5(pk, bp, WINDOW), kv5(pv, bp, WINDOW),
            pc[None, :, SUBLANES - (CONV_WIDTH - 1):],
            kv5(k_s, bs, ts), kv5(v_s, bs, ts),
            gv_s.reshape(1, bs, ts, A_HEADS, A_HEAD_DIM),
            a_s.reshape(bs, ts, D_FF)[None, :, ts - (CONV_WIDTH - 1):])
```

```python
import functools
import math

import jax
import jax.numpy as jnp
import numpy as np
from jax import lax
from jax.experimental import pallas as pl
from jax.experimental.pallas import tpu as pltpu

D_MODEL = 1024
CHUNK = 64
HEAD_DIM = 64
HALF = HEAD_DIM // 2
A_HEADS = 4
A_HEAD_DIM = 128
A_WIDTH = A_HEADS * A_HEAD_DIM
GMLP_CHUNK = 128
B_HEADS = 8
B_KV_HEADS = 2
GQA_GROUP = B_HEADS // B_KV_HEADS
B_WIDTH = B_HEADS * HEAD_DIM
KV_WIDTH = B_KV_HEADS * HEAD_DIM
WINDOW = 128
MIX_WIDTH = A_WIDTH + B_WIDTH
D_FF = 2816
CONV_WIDTH = 3
PAST_LEN = 2048
ROPE_THETA = 10000.0
LN_EPS = 1e-5
RMS_EPS = 1e-6
DEPTH = 1
ALPHA = (2 * DEPTH) ** 0.25
ATTN_SCALE = HEAD_DIM ** -0.5
LOG2_E = math.log2(math.e)
QK_SCALE = ATTN_SCALE * LOG2_E

LANES = 128
SUBLANES = 8
V7X_VMEM_LIMIT_BYTES = 56 * 1024 * 1024

Q_GROUP_WIDTH = GQA_GROUP * HEAD_DIM
KEY_SPAN = 2 * WINDOW
ROW_BLOCK = 128
FF_CHUNK = 256
N_FF_CHUNKS = D_FF // FF_CHUNK
PROJ_BLOCK = 256
DOWN_BLOCK = 256
WOUT_BLOCK = 512
SEQ_TILE = 256
TILES_PER_STEP = 1

F32 = jnp.float32
BF16 = jnp.bfloat16


def _gelu(x):
    k0 = -2.0 * math.sqrt(2.0 / math.pi) * math.log2(math.e)
    k1 = k0 * 0.044715
    return x * (1.0 / (1.0 + jnp.exp2(x * (k0 + k1 * (x * x)))))


def _layer_norm(x, g, b, eps=LN_EPS):
    mu = jnp.mean(x, -1, keepdims=True)
    xc = x - mu
    var = jnp.mean(xc * xc, -1, keepdims=True)
    return xc * lax.rsqrt(var + eps) * g + b


def _post_norm(v, g, b):
    return _layer_norm(v, g, b, eps=LN_EPS / (ALPHA * ALPHA))


def _rms_norm(x, g):
    ms = jnp.mean(x * x, -1, keepdims=True)
    return x * lax.rsqrt(ms + RMS_EPS) * g


def _lane_iota(shape):
    return lax.broadcasted_iota(jnp.int32, shape, 1)


def _row_iota(shape):
    return lax.broadcasted_iota(jnp.int32, shape, 0)


def _rope(x, cos, sin_signed):
    lo = (_lane_iota(cos.shape) % HEAD_DIM) < HALF
    outs = []
    for i in range(x.shape[1] // LANES):
        xs = x[:, i * LANES:(i + 1) * LANES]
        partner = jnp.where(lo, pltpu.roll(xs, LANES - HALF, 1), pltpu.roll(xs, HALF, 1))
        outs.append(xs * cos + partner * sin_signed)
    return outs[0] if len(outs) == 1 else jnp.concatenate(outs, 1)


def _replicate_kv_heads(x):
    lo = _lane_iota(x.shape) < HEAD_DIM
    sw = pltpu.roll(x, HEAD_DIM, 1)
    return jnp.where(lo, x, sw), jnp.where(lo, sw, x)


def _head_lane_masks(rows):
    lane = _lane_iota((rows, Q_GROUP_WIDTH))
    return [jnp.where((lane >= h * HEAD_DIM) & (lane < (h + 1) * HEAD_DIM), 1.0, 0.0).astype(BF16)
            for h in range(GQA_GROUP)]


def _attn_scores(qg, krep, qmasks):
    k2 = jnp.concatenate([krep, krep], 1)
    qm = jnp.concatenate([qg * qmasks[h] for h in range(GQA_GROUP)], 0)
    return lax.dot_general(qm, k2, (((1,), (1,)), ((), ())), preferred_element_type=F32)


def _attn_probs(sc, allowed, sinks):
    r = sc.shape[0] // GQA_GROUP
    ps = []
    for h in range(GQA_GROUP):
        s = jnp.where(allowed, sc[h * r:(h + 1) * r], -jnp.inf)
        sink = sinks[h] * LOG2_E
        m = jnp.maximum(jnp.max(s, -1, keepdims=True), sink)
        e = jnp.exp2(s - m)
        den = jnp.sum(e, -1, keepdims=True) + jnp.exp2(sink - m)
        ps.append((e * (1.0 / den)).astype(BF16))
    return jnp.concatenate(ps, 1)


def _attn_values(pc, vrep, vmasks):
    v2 = jnp.concatenate([vrep, vrep], 1)
    vm = jnp.concatenate([v2 * vmasks[h] for h in range(GQA_GROUP)], 0)
    return jnp.dot(pc, vm, preferred_element_type=F32)


def _attn_block(qg, krep, vrep, allowed, sinks, qmasks, vmasks):
    return _attn_values(_attn_probs(_attn_scores(qg, krep, qmasks), allowed, sinks), vrep, vmasks)


def _mixer_inputs(xb, win_ref, lng, lnb, cos, sin_signed):
    za = jnp.dot(xb, win_ref[:, :2 * A_WIDTH], preferred_element_type=F32)
    u = _gelu(za[:, :A_WIDTH])
    gv = _layer_norm(_gelu(za[:, A_WIDTH:]), lng, lnb)
    zb = jnp.dot(xb, win_ref[:, 2 * A_WIDTH:], preferred_element_type=F32)
    qs = _rope(zb[:, :B_WIDTH], cos, sin_signed)
    k = _rope(zb[:, B_WIDTH:B_WIDTH + KV_WIDTH], cos, sin_signed)
    v = zb[:, B_WIDTH + KV_WIDTH:]
    return u, gv, qs, k, v


def _merge(x, mix, nag, nbg, wout_ref, ln1g, ln1b):
    mi = jnp.concatenate([_rms_norm(mix[:, :A_WIDTH], nag), _rms_norm(mix[:, A_WIDTH:], nbg)], 1)
    m = jnp.dot(mi.astype(BF16), wout_ref[:, :D_MODEL], preferred_element_type=F32)
    return _post_norm(x + m, ln1g, ln1b)


def _conv_taps(a, prev8):
    r1 = pltpu.roll(a, 1, 0)
    r2 = pltpu.roll(a, 2, 0)
    row = _row_iota(prev8.shape)
    first1 = jnp.where(row < 1, pltpu.roll(prev8, 1, 0), r1[:SUBLANES])
    first2 = jnp.where(row < 2, pltpu.roll(prev8, 2, 0), r2[:SUBLANES])
    return (jnp.concatenate([first1, r1[SUBLANES:]], 0), jnp.concatenate([first2, r2[SUBLANES:]], 0))


def _prompt_kernel(sinks_ref, x_ref, cos_ref, sin_ref, win_ref, lng_ref, lnb_ref, ws_ref, bs_ref,
                   nag_ref, nbg_ref, wout_ref, ln1g_ref, ln1b_ref, wg_ref, wu_ref, cw_ref, cb_ref,
                   wd_ref, ln2g_ref, ln2b_ref,
                   y_ref, pk_ref, pv_ref, pc_ref,
                   kv_scr, conv_scr, mix_scr, h_scr, pre1_scr, x1c_scr, x1b_scr, xb_scr, mib_scr,
                   *, tile, seq_tiles, n_steps):
    s = pl.program_id(0)
    mixer_tile0 = TILES_PER_STEP * jnp.minimum(s, n_steps - 2)
    ffn_tile0 = TILES_PER_STEP * jnp.maximum(s - 1, 0)

    @pl.when(s == 0)
    def _():
        pre1_scr[...] = jnp.zeros_like(pre1_scr)

    @pl.when(mixer_tile0 % seq_tiles == 0)
    def _():
        kv_scr[...] = jnp.zeros_like(kv_scr)

    @pl.when(ffn_tile0 % seq_tiles == 0)
    def _():
        conv_scr[...] = jnp.zeros_like(conv_scr)

    row_blocks = [slice(j * ROW_BLOCK, (j + 1) * ROW_BLOCK) for j in range(tile // ROW_BLOCK)]
    attn_blocks = [(j, g) for j in range(tile // ROW_BLOCK) for g in range(B_KV_HEADS)]
    n_proj = win_ref.shape[1] // PROJ_BLOCK
    assert n_proj == 7 and len(row_blocks) == 2 and len(attn_blocks) == 4 and N_FF_CHUNKS == 11

    def pipeline_slot(k):
        tm = (mixer_tile0 + k) % seq_tiles
        trows = slice(k * tile, (k + 1) * tile)
        pre1 = pre1_scr.at[k]
        st = {}

        def proj_cast():
            xb_scr[...] = x_ref[0, trows, :].astype(BF16)

        def proj_dot(i):
            cols = slice(i * PROJ_BLOCK, (i + 1) * PROJ_BLOCK)
            st['z', i] = jnp.dot(xb_scr[...], win_ref[:, cols], preferred_element_type=F32)

        def epi_u(i):
            st['u', i] = _gelu(st.pop(('z', i)))

        def epi_gv():
            g = jnp.concatenate([_gelu(st.pop(('z', 2))), _gelu(st.pop(('z', 3)))], 1)
            st['gvb'] = _layer_norm(g, lng_ref[...], lnb_ref[...]).astype(BF16)

        def rope_tables():
            row0 = pl.multiple_of(tm * tile, tile)
            return cos_ref[pl.ds(row0, tile), :], sin_ref[pl.ds(row0, tile), :]

        def epi_q(i):
            cos, sin_signed = rope_tables()
            st['qb', i - 4] = _rope(st.pop(('z', i)), cos, sin_signed).astype(BF16)

        def epi_kv():
            cos, sin_signed = rope_tables()
            zkv = st.pop(('z', 6))
            kk = _rope(zkv[:, :KV_WIDTH], cos, sin_signed)
            vv = zkv[:, KV_WIDTH:]
            pk_ref[0] = kk[tile - WINDOW:]
            pv_ref[0] = vv[tile - WINDOW:]
            k0, k1 = _replicate_kv_heads(kk)
            v0, v1 = _replicate_kv_heads(vv)
            reps = [jnp.concatenate([kv_scr[i], a.astype(BF16)], 0) for i, a in enumerate((k0, k1, v0, v1))]
            for i in range(4):
                kv_scr[i] = reps[i][tile:]
            st['reps'] = reps

        def gmlp():
            cidx_r = _row_iota((GMLP_CHUNK, GMLP_CHUNK)) // CHUNK
            cidx_c = _lane_iota((GMLP_CHUNK, GMLP_CHUNK)) // CHUNK
            gvb = st.pop('gvb')
            zero = jnp.zeros((GMLP_CHUNK, A_HEAD_DIM), BF16)
            for hp in range(A_HEADS // 2):
                ha, hb = 2 * hp, 2 * hp + 1
                wm = jnp.concatenate([jnp.where(cidx_r >= cidx_c, ws_ref[h], 0.0) for h in (ha, hb)],
                                     1).astype(BF16)
                bias = jnp.concatenate([bs_ref[ha], bs_ref[hb]], 1)
                cols = slice(ha * A_HEAD_DIM, (hb + 1) * A_HEAD_DIM)
                u = st.pop(('u', hp))
                for c in range(tile // GMLP_CHUNK):
                    rows = slice(c * GMLP_CHUNK, (c + 1) * GMLP_CHUNK)
                    ga = gvb[rows, ha * A_HEAD_DIM:(ha + 1) * A_HEAD_DIM]
                    gb = gvb[rows, hb * A_HEAD_DIM:(hb + 1) * A_HEAD_DIM]
                    rhs = jnp.concatenate([jnp.concatenate([ga, zero], 1), jnp.concatenate([zero, gb], 1)], 0)
                    sg = jnp.dot(wm, rhs, preferred_element_type=F32) + bias
                    mix_scr[rows, cols] = u[rows, :] * sg

        def attn_prep():
            qc = _row_iota((ROW_BLOCK, KEY_SPAN)) // CHUNK
            kc = _lane_iota((ROW_BLOCK, KEY_SPAN)) // CHUNK
            in_band = (kc >= qc) & (kc <= qc + WINDOW // CHUNK)
            first_lo = jnp.where(tm > 0, 0, WINDOW // CHUNK)
            st.update(in_band=in_band, allowed_first=in_band & (kc >= first_lo),
                      qmasks=_head_lane_masks(ROW_BLOCK), vmasks=_head_lane_masks(KEY_SPAN))

        def attn_qk(i):
            j, g = attn_blocks[i]
            qg = st['qb', g][row_blocks[j], :]
            st['sc', i] = _attn_scores(qg, st['reps'][g][j * ROW_BLOCK:j * ROW_BLOCK + KEY_SPAN],
                                       st['qmasks'])

        def attn_softmax(i):
            j, g = attn_blocks[i]
            sinks = [sinks_ref[g * GQA_GROUP + h] for h in range(GQA_GROUP)]
            allowed = st['allowed_first'] if j == 0 else st['in_band']
            st['p', i] = _attn_probs(st.pop(('sc', i)), allowed, sinks)

        def attn_pv(i):
            j, g = attn_blocks[i]
            out = _attn_values(st.pop(('p', i)),
                               st['reps'][2 + g][j * ROW_BLOCK:j * ROW_BLOCK + KEY_SPAN], st['vmasks'])
            mix_scr[row_blocks[j], A_WIDTH + g * Q_GROUP_WIDTH:A_WIDTH + (g + 1) * Q_GROUP_WIDTH] = out

        def rms(j):
            rows = row_blocks[j]
            mi = jnp.concatenate([_rms_norm(mix_scr[rows, :A_WIDTH], nag_ref[...]),
                                  _rms_norm(mix_scr[rows, A_WIDTH:], nbg_ref[...])], 1)
            mib_scr[rows, :] = mi.astype(BF16)

        def wout(nb):
            cols = slice(nb * WOUT_BLOCK, (nb + 1) * WOUT_BLOCK)
            m = jnp.dot(mib_scr[...], wout_ref[:, cols], preferred_element_type=F32)
            pre1[:, cols] = x_ref[0, trows, cols] + m

        def ln1(j):
            rows = row_blocks[j]
            x1 = _post_norm(pre1[rows, :], ln1g_ref[...], ln1b_ref[...])
            x1c_scr[rows, :] = x1
            x1b_scr[rows, :] = x1.astype(BF16)

        def ffn_dots(c):
            cs = slice(c * FF_CHUNK, (c + 1) * FF_CHUNK)
            x1b = x1b_scr[...]
            st['a', c] = jnp.dot(x1b, wg_ref[:, cs], preferred_element_type=F32)
            st['up', c] = jnp.dot(x1b, wu_ref[:, cs], preferred_element_type=F32)

        def ffn_epi(c):
            cs = slice(c * FF_CHUNK, (c + 1) * FF_CHUNK)
            a, up = st.pop(('a', c)), st.pop(('up', c))
            a1, a2 = _conv_taps(a, conv_scr[:, cs])
            cc = a2 * cw_ref[0:1, cs] + a1 * cw_ref[1:2, cs] + a * cw_ref[2:3, cs] + cb_ref[:, cs]
            h_scr[:, cs] = (_gelu(cc) * up).astype(BF16)
            conv_scr[:, cs] = a[tile - SUBLANES:]

        def down(nb):
            cols = slice(nb * DOWN_BLOCK, (nb + 1) * DOWN_BLOCK)
            f = jnp.dot(h_scr[...], wd_ref[:, cols], preferred_element_type=F32)
            st['pre2', nb] = x1c_scr[:, cols] + f

        def ln2():
            pre2 = jnp.concatenate([st.pop(('pre2', nb)) for nb in range(D_MODEL // DOWN_BLOCK)], 1)
            y_ref[0, trows, :] = _post_norm(pre2, ln2g_ref[...], ln2b_ref[...])

        proj_cast()
        proj_dot(0)
        ln1(0)
        proj_dot(1)
        ln1(1)
        proj_dot(2)
        proj_dot(3)
        proj_dot(4)
        epi_u(0)
        ffn_dots(0)
        epi_u(1)
        ffn_dots(1)
        ffn_epi(0)
        proj_dot(5)
        ffn_dots(2)
        ffn_epi(1)
        epi_gv()
        proj_dot(6)
        ffn_dots(3)
        ffn_epi(2)
        epi_q(4)
        ffn_dots(4)
        ffn_epi(3)
        gmlp()
        epi_q(5)
        ffn_dots(5)
        ffn_epi(4)
        epi_kv()
        attn_prep()
        for c in range(6, N_FF_CHUNKS):
            ffn_dots(c)
            ffn_epi(c - 1)
        attn_qk(0)
        attn_qk(1)
        ffn_epi(N_FF_CHUNKS - 1)
        attn_qk(2)
        attn_qk(3)
        down(0)
        attn_softmax(0)
        attn_softmax(1)
        down(1)
        attn_pv(0)
        attn_pv(1)
        attn_softmax(2)
        attn_softmax(3)
        down(2)
        attn_pv(2)
        attn_pv(3)
        rms(0)
        down(3)
        rms(1)
        wout(0)
        ln2()
        wout(1)

    for k in range(TILES_PER_STEP):
        pipeline_slot(k)
    pc_ref[0] = conv_scr[...]


def _sample_kernel(sinks_ref, x_ref, cos_ref, sin_ref, ck_ref, cv_ref, s1_ref, s2_ref,
                   win_ref, lng_ref, lnb_ref, wt_ref, bs_ref,
                   nag_ref, nbg_ref, wout_ref, ln1g_ref, ln1b_ref, wg_ref, wu_ref, cw_ref, cb_ref,
                   wd_ref, ln2g_ref, ln2b_ref,
                   y_ref, k_ref, v_ref, gv_ref, a_ref,
                   mix_scr, h_scr, *, n_batch, s_len):
    rows_all = n_batch * s_len
    x = x_ref[...]
    xb = x.astype(BF16)
    u, gv, q, k, v = _mixer_inputs(xb, win_ref, lng_ref[...], lnb_ref[...], cos_ref[...], sin_ref[...])
    gv_ref[...] = gv
    k_ref[...] = k
    v_ref[...] = v

    ri = _row_iota((rows_all, rows_all))
    ci = _lane_iota((rows_all, rows_all))
    same_batch = (ri // s_len) == (ci // s_len)
    causal = ((ri % s_len) // CHUNK) >= ((ci % s_len) // CHUNK)
    gvb = gv.astype(BF16)
    for h in range(A_HEADS):
        wm = jnp.where(same_batch & causal, wt_ref[h], 0.0).astype(BF16)
        cols = slice(h * A_HEAD_DIM, (h + 1) * A_HEAD_DIM)
        s = jnp.dot(wm, gvb[:, cols], preferred_element_type=F32) + bs_ref[h]
        mix_scr[:, cols] = u[:, cols] * s

    qb = q.astype(BF16)
    qmasks = _head_lane_masks(s_len)
    vmasks = _head_lane_masks(KEY_SPAN)
    allowed = _lane_iota((s_len, KEY_SPAN)) < WINDOW + s_len
    pad = jnp.zeros((KEY_SPAN - WINDOW - s_len, LANES), F32)
    for b in range(n_batch):
        rows = slice(b * s_len, (b + 1) * s_len)
        k_all = jnp.concatenate([ck_ref[b], k[rows], pad], 0)
        v_all = jnp.concatenate([cv_ref[b], v[rows], pad], 0)
        kreps = [a.astype(BF16) for a in _replicate_kv_heads(k_all)]
        vreps = [a.astype(BF16) for a in _replicate_kv_heads(v_all)]
        for g in range(B_KV_HEADS):
            sinks = [sinks_ref[g * GQA_GROUP + h] for h in range(GQA_GROUP)]
            qcols = slice(g * Q_GROUP_WIDTH, (g + 1) * Q_GROUP_WIDTH)
            out = _attn_block(qb[rows, qcols], kreps[g], vreps[g], allowed, sinks, qmasks, vmasks)
            mix_scr[rows, A_WIDTH + g * Q_GROUP_WIDTH:A_WIDTH + (g + 1) * Q_GROUP_WIDTH] = out

    x1 = _merge(x, mix_scr[...], nag_ref[...], nbg_ref[...], wout_ref, ln1g_ref[...], ln1b_ref[...])
    x1b = x1.astype(BF16)

    pos = _row_iota((rows_all, FF_CHUNK)) % s_len
    for c in range(N_FF_CHUNKS):
        cs = slice(c * FF_CHUNK, (c + 1) * FF_CHUNK)
        a = jnp.dot(x1b, wg_ref[:, cs], preferred_element_type=F32)
        up = jnp.dot(x1b, wu_ref[:, cs], preferred_element_type=F32)
        a_ref[:, cs] = a
        a1 = jnp.where(pos < 1, s1_ref[:, cs], pltpu.roll(a, 1, 0))
        a2 = jnp.where(pos < 2, s2_ref[:, cs], pltpu.roll(a, 2, 0))
        cc = a2 * cw_ref[0:1, cs] + a1 * cw_ref[1:2, cs] + a * cw_ref[2:3, cs] + cb_ref[:, cs]
        h_scr[:, cs] = (_gelu(cc) * up).astype(BF16)
    f = jnp.dot(h_scr[...], wd_ref[:, :D_MODEL], preferred_element_type=F32)
    y_ref[...] = _post_norm(x1 + f, ln2g_ref[...], ln2b_ref[...])


def _rope_tables(pos):
    inv = ROPE_THETA ** (-np.arange(HALF, dtype=np.float64) / HALF)
    ang = pos.astype(np.float64)[:, None] * inv[None, :]
    cos, sin = np.cos(ang), np.sin(ang)
    reps = LANES // HEAD_DIM
    return (np.tile(np.concatenate([cos, cos], -1), (1, reps)).astype(np.float32),
            np.tile(np.concatenate([-sin, sin], -1), (1, reps)).astype(np.float32))


def _resident(shape):
    return pl.BlockSpec(shape, lambda *_: (0,) * len(shape), pipeline_mode=pl.Buffered(1))


def kernel(x_prompt, x_sample, cache_k, cache_v, state_ffn_conv, w_in, gmlp_ln_g, gmlp_ln_b,
           gmlp_w_s, gmlp_b_s, attn_sinks, norm_a_g, norm_b_g, w_out, ln1_g, ln1_b,
           w_gate, w_up, conv_w, conv_b, w_down, ln2_g, ln2_b):
    assert w_in.shape[0] == DEPTH == 1
    bp, tp, _ = x_prompt.shape
    bs, ts, _ = x_sample.shape
    tile = SEQ_TILE
    step_rows = TILES_PER_STEP * tile
    assert tp % step_rows == 0 and tile % ROW_BLOCK == 0 and tile >= WINDOW
    assert bs * ts == ROW_BLOCK and WINDOW + ts <= KEY_SPAN and ts >= CONV_WIDTH - 1

    row = lambda a: a[0].reshape(1, -1)
    pad_lanes = lambda w: jnp.pad(w, ((0, 0), (0, LANES)))
    q_cols = (jnp.arange(w_in.shape[-1]) >= 2 * A_WIDTH) & (jnp.arange(w_in.shape[-1]) < 2 * A_WIDTH + B_WIDTH)
    win_b = (w_in[0] * jnp.where(q_cols, QK_SCALE, 1.0)[None, :]).astype(BF16)
    wout_b = pad_lanes((w_out[0] * (1.0 / ALPHA)).astype(BF16))
    wg_b, wu_b = w_gate[0].astype(BF16), w_up[0].astype(BF16)
    wd_b = pad_lanes((w_down[0] * (1.0 / ALPHA)).astype(BF16))
    sinks = attn_sinks[0]
    vec_args = dict(lng=row(gmlp_ln_g), lnb=row(gmlp_ln_b), nag=row(norm_a_g), nbg=row(norm_b_g),
                    ln1g=row(ln1_g), ln1b=row(ln1_b), cb=row(conv_b), ln2g=row(ln2_g), ln2b=row(ln2_b))
    cw = conv_w[0]
    smem = pl.BlockSpec(memory_space=pltpu.SMEM)

    def weight_specs(ws_shape):
        return [_resident(win_b.shape), _resident((1, A_WIDTH)), _resident((1, A_WIDTH)),
                _resident(ws_shape), _resident(ws_shape),
                _resident((1, A_WIDTH)), _resident((1, B_WIDTH)), _resident(wout_b.shape),
                _resident((1, D_MODEL)), _resident((1, D_MODEL)),
                _resident(wg_b.shape), _resident(wu_b.shape), _resident(cw.shape), _resident((1, D_FF)),
                _resident(wd_b.shape), _resident((1, D_MODEL)), _resident((1, D_MODEL))]

    def weight_args(ws, bsb):
        return (win_b, vec_args['lng'], vec_args['lnb'], ws, bsb, vec_args['nag'], vec_args['nbg'],
                wout_b, vec_args['ln1g'], vec_args['ln1b'], wg_b, wu_b, cw, vec_args['cb'], wd_b,
                vec_args['ln2g'], vec_args['ln2b'])

    cos_p, sin_p = _rope_tables(np.arange(tp))
    bias_p = jnp.broadcast_to(gmlp_b_s[0][:, :, None], (A_HEADS, GMLP_CHUNK, A_HEAD_DIM))
    seq_tiles = tp // tile
    seq_blocks = tp // step_rows
    n_blocks = bp * seq_blocks
    n_steps = n_blocks + 1
    def stage_bt(lag):
        def bt(s):
            i = jnp.clip(s - lag, 0, n_blocks - 1)
            return i // seq_blocks, i % seq_blocks
        return bt
    mixer_bt, ffn_bt = stage_bt(0), stage_bt(1)
    y_p, pk, pv, pc = pl.pallas_call(
        functools.partial(_prompt_kernel, tile=tile, seq_tiles=seq_tiles, n_steps=n_steps),
        grid=(n_steps,),
        in_specs=[smem,
                  pl.BlockSpec((1, step_rows, D_MODEL), lambda s: (*mixer_bt(s), 0)),
                  _resident(cos_p.shape), _resident(sin_p.shape)]
                 + weight_specs((A_HEADS, GMLP_CHUNK, GMLP_CHUNK)),
        out_specs=[pl.BlockSpec((1, step_rows, D_MODEL), lambda s: (*ffn_bt(s), 0)),
                   pl.BlockSpec((1, WINDOW, KV_WIDTH), lambda s: (mixer_bt(s)[0], 0, 0)),
                   pl.BlockSpec((1, WINDOW, KV_WIDTH), lambda s: (mixer_bt(s)[0], 0, 0)),
                   pl.BlockSpec((1, SUBLANES, D_FF), lambda s: (ffn_bt(s)[0], 0, 0))],
        out_shape=[jax.ShapeDtypeStruct((bp, tp, D_MODEL), F32),
                   jax.ShapeDtypeStruct((bp, WINDOW, KV_WIDTH), F32),
                   jax.ShapeDtypeStruct((bp, WINDOW, KV_WIDTH), F32),
                   jax.ShapeDtypeStruct((bp, SUBLANES, D_FF), F32)],
        scratch_shapes=[pltpu.VMEM((4, WINDOW, LANES), BF16),
                        pltpu.VMEM((SUBLANES, D_FF), F32),
                        pltpu.VMEM((tile, MIX_WIDTH), F32),
                        pltpu.VMEM((tile, D_FF), BF16),
                        pltpu.VMEM((TILES_PER_STEP, tile, D_MODEL), F32),
                        pltpu.VMEM((tile, D_MODEL), F32),
                        pltpu.VMEM((tile, D_MODEL), BF16),
                        pltpu.VMEM((tile, D_MODEL), BF16),
                        pltpu.VMEM((tile, MIX_WIDTH), BF16)],
        compiler_params=pltpu.CompilerParams(dimension_semantics=("arbitrary",),
                                             vmem_limit_bytes=V7X_VMEM_LIMIT_BYTES),
        name="prompt_layer",
    )(sinks, x_prompt, cos_p, sin_p, *weight_args(gmlp_w_s[0], bias_p))

    rows_all = bs * ts
    cos_s, sin_s = _rope_tables(PAST_LEN + np.arange(ts))
    cos_s, sin_s = np.tile(cos_s, (bs, 1)), np.tile(sin_s, (bs, 1))
    ck = cache_k[0].reshape(bs, WINDOW, KV_WIDTH)
    cv = cache_v[0].reshape(bs, WINDOW, KV_WIDTH)
    st = state_ffn_conv[0]
    pad_frames = lambda a: jnp.pad(a, ((0, 0), (0, ts - a.shape[1]), (0, 0))).reshape(rows_all, D_FF)
    s1, s2 = pad_frames(st[:, 1:]), pad_frames(st)
    w_tiled = jnp.tile(gmlp_w_s[0][:, :ts, :ts], (1, bs, bs))
    bias_s = jnp.broadcast_to(jnp.tile(gmlp_b_s[0][:, :ts], (1, bs))[:, :, None],
                              (A_HEADS, rows_all, A_HEAD_DIM))
    full = lambda shape: pl.BlockSpec(shape, lambda i: (0,) * len(shape))
    y_s, k_s, v_s, gv_s, a_s = pl.pallas_call(
        functools.partial(_sample_kernel, n_batch=bs, s_len=ts),
        grid=(1,),
        in_specs=[smem, _resident((rows_all, D_MODEL)), _resident(cos_s.shape), _resident(sin_s.shape),
                  _resident(ck.shape), _resident(cv.shape), _resident(s1.shape), _resident(s2.shape)]
                 + weight_specs((A_HEADS, rows_all, rows_all)),
        out_specs=[full((rows_all, D_MODEL)), full((rows_all, KV_WIDTH)), full((rows_all, KV_WIDTH)),
                   full((rows_all, A_WIDTH)), full((rows_all, D_FF))],
        out_shape=[jax.ShapeDtypeStruct((rows_all, D_MODEL), F32),
                   jax.ShapeDtypeStruct((rows_all, KV_WIDTH), F32),
                   jax.ShapeDtypeStruct((rows_all, KV_WIDTH), F32),
                   jax.ShapeDtypeStruct((rows_all, A_WIDTH), F32),
                   jax.ShapeDtypeStruct((rows_all, D_FF), F32)],
        scratch_shapes=[pltpu.VMEM((rows_all, MIX_WIDTH), F32),
                        pltpu.VMEM((rows_all, D_FF), BF16)],
        compiler_params=pltpu.CompilerParams(dimension_semantics=("arbitrary",),
                                             vmem_limit_bytes=V7X_VMEM_LIMIT_BYTES),
        name="sample_layer",
    )(sinks, x_sample.reshape(rows_all, D_MODEL), cos_s, sin_s, ck, cv, s1, s2,
      *weight_args(w_tiled, bias_s))

    kv5 = lambda a, n, t: a.reshape(1, n, t, B_KV_HEADS, HEAD_DIM)
    return (y_p, y_s.reshape(bs, ts, D_MODEL),
            kv5(pk, bp, WINDOW), kv5(pv, bp, WINDOW),
            pc[None, :, SUBLANES - (CONV_WIDTH - 1):],
            kv5(k_s, bs, ts), kv5(v_s, bs, ts),
            gv_s.reshape(1, bs, ts, A_HEADS, A_HEAD_DIM),
            a_s.reshape(bs, ts, D_FF)[None, :, ts - (CONV_WIDTH - 1):])
```

```python
import functools
import math

import jax
import jax.numpy as jnp
import numpy as np
from jax import lax
from jax.experimental import pallas as pl
from jax.experimental.pallas import tpu as pltpu

D_MODEL = 1024
CHUNK = 64
HEAD_DIM = 64
HALF = HEAD_DIM // 2
A_HEADS = 4
A_HEAD_DIM = 128
A_WIDTH = A_HEADS * A_HEAD_DIM
GMLP_CHUNK = 128
B_HEADS = 8
B_KV_HEADS = 2
GQA_GROUP = B_HEADS // B_KV_HEADS
B_WIDTH = B_HEADS * HEAD_DIM
KV_WIDTH = B_KV_HEADS * HEAD_DIM
WINDOW = 128
MIX_WIDTH = A_WIDTH + B_WIDTH
D_FF = 2816
CONV_WIDTH = 3
PAST_LEN = 2048
ROPE_THETA = 10000.0
LN_EPS = 1e-5
RMS_EPS = 1e-6
DEPTH = 1
ALPHA = (2 * DEPTH) ** 0.25
ATTN_SCALE = HEAD_DIM ** -0.5
LOG2_E = math.log2(math.e)
QK_SCALE = ATTN_SCALE * LOG2_E

LANES = 128
SUBLANES = 8
V7X_VMEM_LIMIT_BYTES = 56 * 1024 * 1024

Q_GROUP_WIDTH = GQA_GROUP * HEAD_DIM
KEY_SPAN = 2 * WINDOW
ROW_BLOCK = 128
FF_CHUNK = 256
N_FF_CHUNKS = D_FF // FF_CHUNK
PROJ_BLOCK = 256
DOWN_BLOCK = 256
WOUT_BLOCK = 512
SEQ_TILE = 256
TILES_PER_STEP = 1

F32 = jnp.float32
BF16 = jnp.bfloat16


def _gelu(x):
    k0 = -2.0 * math.sqrt(2.0 / math.pi) * math.log2(math.e)
    k1 = k0 * 0.044715
    return x * (1.0 / (1.0 + jnp.exp2(x * (k0 + k1 * (x * x)))))


def _layer_norm(x, g, b, eps=LN_EPS):
    mu = jnp.mean(x, -1, keepdims=True)
    xc = x - mu
    var = jnp.mean(xc * xc, -1, keepdims=True)
    return xc * lax.rsqrt(var + eps) * g + b


def _post_norm(v, g, b):
    return _layer_norm(v, g, b, eps=LN_EPS / (ALPHA * ALPHA))


def _rms_norm(x, g):
    ms = jnp.mean(x * x, -1, keepdims=True)
    return x * lax.rsqrt(ms + RMS_EPS) * g


def _lane_iota(shape):
    return lax.broadcasted_iota(jnp.int32, shape, 1)


def _row_iota(shape):
    return lax.broadcasted_iota(jnp.int32, shape, 0)


def _rope(x, cos, sin_signed):
    lo = (_lane_iota(cos.shape) % HEAD_DIM) < HALF
    outs = []
    for i in range(x.shape[1] // LANES):
        xs = x[:, i * LANES:(i + 1) * LANES]
        partner = jnp.where(lo, pltpu.roll(xs, LANES - HALF, 1), pltpu.roll(xs, HALF, 1))
        outs.append(xs * cos + partner * sin_signed)
    return outs[0] if len(outs) == 1 else jnp.concatenate(outs, 1)


def _replicate_kv_heads(x):
    lo = _lane_iota(x.shape) < HEAD_DIM
    sw = pltpu.roll(x, HEAD_DIM, 1)
    return jnp.where(lo, x, sw), jnp.where(lo, sw, x)


def _head_lane_masks(rows):
    lane = _lane_iota((rows, Q_GROUP_WIDTH))
    return [jnp.where((lane >= h * HEAD_DIM) & (lane < (h + 1) * HEAD_DIM), 1.0, 0.0).astype(BF16)
            for h in range(GQA_GROUP)]


def _attn_scores(qg, krep, qmasks):
    k2 = jnp.concatenate([krep, krep], 1)
    qm = jnp.concatenate([qg * qmasks[h] for h in range(GQA_GROUP)], 0)
    return lax.dot_general(qm, k2, (((1,), (1,)), ((), ())), preferred_element_type=F32)


def _attn_probs(sc, allowed, sinks):
    r = sc.shape[0] // GQA_GROUP
    ps = []
    for h in range(GQA_GROUP):
        s = jnp.where(allowed, sc[h * r:(h + 1) * r], -jnp.inf)
        sink = sinks[h] * LOG2_E
        m = jnp.maximum(jnp.max(s, -1, keepdims=True), sink)
        e = jnp.exp2(s - m)
        den = jnp.sum(e, -1, keepdims=True) + jnp.exp2(sink - m)
        ps.append((e * (1.0 / den)).astype(BF16))
    return jnp.concatenate(ps, 1)


def _attn_values(pc, vrep, vmasks):
    v2 = jnp.concatenate([vrep, vrep], 1)
    vm = jnp.concatenate([v2 * vmasks[h] for h in range(GQA_GROUP)], 0)
    return jnp.dot(pc, vm, preferred_element_type=F32)


def _attn_block(qg, krep, vrep, allowed, sinks, qmasks, vmasks):
    return _attn_values(_attn_probs(_attn_scores(qg, krep, qmasks), allowed, sinks), vrep, vmasks)


def _mixer_inputs(xb, win_ref, lng, lnb, cos, sin_signed):
    za = jnp.dot(xb, win_ref[:, :2 * A_WIDTH], preferred_element_type=F32)
    u = _gelu(za[:, :A_WIDTH])
    gv = _layer_norm(_gelu(za[:, A_WIDTH:]), lng, lnb)
    zb = jnp.dot(xb, win_ref[:, 2 * A_WIDTH:], preferred_element_type=F32)
    qs = _rope(zb[:, :B_WIDTH], cos, sin_signed)
    k = _rope(zb[:, B_WIDTH:B_WIDTH + KV_WIDTH], cos, sin_signed)
    v = zb[:, B_WIDTH + KV_WIDTH:]
    return u, gv, qs, k, v


def _merge(x, mix, nag, nbg, wout_ref, ln1g, ln1b):
    mi = jnp.concatenate([_rms_norm(mix[:, :A_WIDTH], nag), _rms_norm(mix[:, A_WIDTH:], nbg)], 1)
    m = jnp.dot(mi.astype(BF16), wout_ref[:, :D_MODEL], preferred_element_type=F32)
    return _post_norm(x + m, ln1g, ln1b)


def _conv_taps(a, prev8):
    r1 = pltpu.roll(a, 1, 0)
    r2 = pltpu.roll(a, 2, 0)
    row = _row_iota(prev8.shape)
    first1 = jnp.where(row < 1, pltpu.roll(prev8, 1, 0), r1[:SUBLANES])
    first2 = jnp.where(row < 2, pltpu.roll(prev8, 2, 0), r2[:SUBLANES])
    return (jnp.concatenate([first1, r1[SUBLANES:]], 0), jnp.concatenate([first2, r2[SUBLANES:]], 0))


def _prompt_kernel(sinks_ref, x_ref, cos_ref, sin_ref, win_ref, lng_ref, lnb_ref, ws_ref, bs_ref,
                   nag_ref, nbg_ref, wout_ref, ln1g_ref, ln1b_ref, wg_ref, wu_ref, cw_ref, cb_ref,
                   wd_ref, ln2g_ref, ln2b_ref,
                   y_ref, pk_ref, pv_ref, pc_ref,
                   kv_scr, conv_scr, mix_scr, h_scr, pre1_scr, x1c_scr, x1b_scr, xb_scr, mib_scr,
                   *, tile, seq_tiles, n_steps):
    s = pl.program_id(0)
    mixer_tile0 = TILES_PER_STEP * jnp.minimum(s, n_steps - 2)
    ffn_tile0 = TILES_PER_STEP * jnp.maximum(s - 1, 0)

    @pl.when(s == 0)
    def _():
        pre1_scr[...] = jnp.zeros_like(pre1_scr)

    @pl.when(mixer_tile0 % seq_tiles == 0)
    def _():
        kv_scr[...] = jnp.zeros_like(kv_scr)

    @pl.when(ffn_tile0 % seq_tiles == 0)
    def _():
        conv_scr[...] = jnp.zeros_like(conv_scr)

    row_blocks = [slice(j * ROW_BLOCK, (j + 1) * ROW_BLOCK) for j in range(tile // ROW_BLOCK)]
    attn_blocks = [(j, g) for j in range(tile // ROW_BLOCK) for g in range(B_KV_HEADS)]
    n_proj = win_ref.shape[1] // PROJ_BLOCK
    assert n_proj == 7 and len(row_blocks) == 2 and len(attn_blocks) == 4 and N_FF_CHUNKS == 11

    def pipeline_slot(k):
        tm = (mixer_tile0 + k) % seq_tiles
        trows = slice(k * tile, (k + 1) * tile)
        pre1 = pre1_scr.at[k]
        st = {}

        def proj_cast():
            xb_scr[:, :D_MODEL] = x_ref[0, trows, :].astype(BF16)

        def proj_dot(i):
            cols = slice(i * PROJ_BLOCK, (i + 1) * PROJ_BLOCK)
            st['z', i] = jnp.dot(xb_scr[:, :D_MODEL], win_ref[:, cols], preferred_element_type=F32)

        def epi_u(i):
            st['u', i] = _gelu(st.pop(('z', i)))

        def epi_gv():
            g = jnp.concatenate([_gelu(st.pop(('z', 2))), _gelu(st.pop(('z', 3)))], 1)
            st['gvb'] = _layer_norm(g, lng_ref[...], lnb_ref[...]).astype(BF16)

        def rope_tables():
            row0 = pl.multiple_of(tm * tile, tile)
            return cos_ref[pl.ds(row0, tile), :], sin_ref[pl.ds(row0, tile), :]

        def epi_q(i):
            cos, sin_signed = rope_tables()
            st['qb', i - 4] = _rope(st.pop(('z', i)), cos, sin_signed).astype(BF16)

        def epi_kv():
            cos, sin_signed = rope_tables()
            zkv = st.pop(('z', 6))
            kk = _rope(zkv[:, :KV_WIDTH], cos, sin_signed)
            vv = zkv[:, KV_WIDTH:]
            pk_ref[0] = kk[tile - WINDOW:]
            pv_ref[0] = vv[tile - WINDOW:]
            k0, k1 = _replicate_kv_heads(kk)
            v0, v1 = _replicate_kv_heads(vv)
            reps = [jnp.concatenate([kv_scr[i], a.astype(BF16)], 0) for i, a in enumerate((k0, k1, v0, v1))]
            for i in range(4):
                kv_scr[i] = reps[i][tile:]
            st['reps'] = reps

        def gmlp():
            cidx_r = _row_iota((GMLP_CHUNK, GMLP_CHUNK)) // CHUNK
            cidx_c = _lane_iota((GMLP_CHUNK, GMLP_CHUNK)) // CHUNK
            gvb = st.pop('gvb')
            zero = jnp.zeros((GMLP_CHUNK, A_HEAD_DIM), BF16)
            for hp in range(A_HEADS // 2):
                ha, hb = 2 * hp, 2 * hp + 1
                wm = jnp.concatenate([jnp.where(cidx_r >= cidx_c, ws_ref[h], 0.0) for h in (ha, hb)],
                                     1).astype(BF16)
                bias = jnp.concatenate([bs_ref[ha], bs_ref[hb]], 1)
                cols = slice(ha * A_HEAD_DIM, (hb + 1) * A_HEAD_DIM)
                u = st.pop(('u', hp))
                for c in range(tile // GMLP_CHUNK):
                    rows = slice(c * GMLP_CHUNK, (c + 1) * GMLP_CHUNK)
                    ga = gvb[rows, ha * A_HEAD_DIM:(ha + 1) * A_HEAD_DIM]
                    gb = gvb[rows, hb * A_HEAD_DIM:(hb + 1) * A_HEAD_DIM]
                    rhs = jnp.concatenate([jnp.concatenate([ga, zero], 1), jnp.concatenate([zero, gb], 1)], 0)
                    sg = jnp.dot(wm, rhs, preferred_element_type=F32) + bias
                    mix_scr[rows, cols] = u[rows, :] * sg

        def attn_prep():
            qc = _row_iota((ROW_BLOCK, KEY_SPAN)) // CHUNK
            kc = _lane_iota((ROW_BLOCK, KEY_SPAN)) // CHUNK
            in_band = (kc >= qc) & (kc <= qc + WINDOW // CHUNK)
            first_lo = jnp.where(tm > 0, 0, WINDOW // CHUNK)
            st.update(in_band=in_band, allowed_first=in_band & (kc >= first_lo),
                      qmasks=_head_lane_masks(ROW_BLOCK), vmasks=_head_lane_masks(KEY_SPAN))

        def attn_qk(i):
            j, g = attn_blocks[i]
            qg = st['qb', g][row_blocks[j], :]
            st['sc', i] = _attn_scores(qg, st['reps'][g][j * ROW_BLOCK:j * ROW_BLOCK + KEY_SPAN],
                                       st['qmasks'])

        def attn_softmax(i):
            j, g = attn_blocks[i]
            sinks = [sinks_ref[g * GQA_GROUP + h] for h in range(GQA_GROUP)]
            allowed = st['allowed_first'] if j == 0 else st['in_band']
            st['p', i] = _attn_probs(st.pop(('sc', i)), allowed, sinks)

        def attn_pv(i):
            j, g = attn_blocks[i]
            out = _attn_values(st.pop(('p', i)),
                               st['reps'][2 + g][j * ROW_BLOCK:j * ROW_BLOCK + KEY_SPAN], st['vmasks'])
            mix_scr[row_blocks[j], A_WIDTH + g * Q_GROUP_WIDTH:A_WIDTH + (g + 1) * Q_GROUP_WIDTH] = out

        def rms(j):
            rows = row_blocks[j]
            mi = jnp.concatenate([_rms_norm(mix_scr[rows, :A_WIDTH], nag_ref[...]),
                                  _rms_norm(mix_scr[rows, A_WIDTH:], nbg_ref[...])], 1)
            mib_scr[rows, :MIX_WIDTH] = mi.astype(BF16)

        def wout(nb):
            cols = slice(nb * WOUT_BLOCK, (nb + 1) * WOUT_BLOCK)
            m = jnp.dot(mib_scr[:, :MIX_WIDTH], wout_ref[:, cols], preferred_element_type=F32)
            pre1[:, cols] = x_ref[0, trows, cols] + m

        def ln1(j):
            rows = row_blocks[j]
            x1 = _post_norm(pre1[rows, :], ln1g_ref[...], ln1b_ref[...])
            x1c_scr[rows, :] = x1
            x1b_scr[rows, :D_MODEL] = x1.astype(BF16)

        def ffn_dots(c):
            cs = slice(c * FF_CHUNK, (c + 1) * FF_CHUNK)
            x1b = x1b_scr[:, :D_MODEL]
            st['a', c] = jnp.dot(x1b, wg_ref[:, cs], preferred_element_type=F32)
            st['up', c] = jnp.dot(x1b, wu_ref[:, cs], preferred_element_type=F32)

        def ffn_epi(c):
            cs = slice(c * FF_CHUNK, (c + 1) * FF_CHUNK)
            a, up = st.pop(('a', c)), st.pop(('up', c))
            a1, a2 = _conv_taps(a, conv_scr[:, cs])
            cc = a2 * cw_ref[0:1, cs] + a1 * cw_ref[1:2, cs] + a * cw_ref[2:3, cs] + cb_ref[:, cs]
            h_scr[:, cs] = (_gelu(cc) * up).astype(BF16)
            conv_scr[:, cs] = a[tile - SUBLANES:]

        def down(nb):
            cols = slice(nb * DOWN_BLOCK, (nb + 1) * DOWN_BLOCK)
            f = jnp.dot(h_scr[...], wd_ref[:, cols], preferred_element_type=F32)
            st['pre2', nb] = x1c_scr[:, cols] + f

        def ln2():
            pre2 = jnp.concatenate([st.pop(('pre2', nb)) for nb in range(D_MODEL // DOWN_BLOCK)], 1)
            y_ref[0, trows, :] = _post_norm(pre2, ln2g_ref[...], ln2b_ref[...])

        proj_cast()
        proj_dot(0)
        ln1(0)
        proj_dot(1)
        ln1(1)
        proj_dot(2)
        proj_dot(3)
        proj_dot(4)
        epi_u(0)
        ffn_dots(0)
        epi_u(1)
        ffn_dots(1)
        ffn_epi(0)
        proj_dot(5)
        ffn_dots(2)
        ffn_epi(1)
        epi_gv()
        proj_dot(6)
        ffn_dots(3)
        ffn_epi(2)
        epi_q(4)
        ffn_dots(4)
        ffn_epi(3)
        gmlp()
        epi_q(5)
        ffn_dots(5)
        ffn_epi(4)
        epi_kv()
        attn_prep()
        for c in range(6, N_FF_CHUNKS):
            ffn_dots(c)
            ffn_epi(c - 1)
        attn_qk(0)
        attn_qk(1)
        ffn_epi(N_FF_CHUNKS - 1)
        attn_qk(2)
        attn_qk(3)
        down(0)
        attn_softmax(0)
        attn_softmax(1)
        down(1)
        attn_pv(0)
        attn_pv(1)
        attn_softmax(2)
        attn_softmax(3)
        down(2)
        attn_pv(2)
        attn_pv(3)
        rms(0)
        down(3)
        rms(1)
        wout(0)
        ln2()
        wout(1)

    for k in range(TILES_PER_STEP):
        pipeline_slot(k)
    pc_ref[0] = conv_scr[...]


def _sample_kernel(sinks_ref, x_ref, cos_ref, sin_ref, ck_ref, cv_ref, s1_ref, s2_ref,
                   win_ref, lng_ref, lnb_ref, wt_ref, bs_ref,
                   nag_ref, nbg_ref, wout_ref, ln1g_ref, ln1b_ref, wg_ref, wu_ref, cw_ref, cb_ref,
                   wd_ref, ln2g_ref, ln2b_ref,
                   y_ref, k_ref, v_ref, gv_ref, a_ref,
                   mix_scr, h_scr, *, n_batch, s_len):
    rows_all = n_batch * s_len
    x = x_ref[...]
    xb = x.astype(BF16)
    u, gv, q, k, v = _mixer_inputs(xb, win_ref, lng_ref[...], lnb_ref[...], cos_ref[...], sin_ref[...])
    gv_ref[...] = gv
    k_ref[...] = k
    v_ref[...] = v

    ri = _row_iota((rows_all, rows_all))
    ci = _lane_iota((rows_all, rows_all))
    same_batch = (ri // s_len) == (ci // s_len)
    causal = ((ri % s_len) // CHUNK) >= ((ci % s_len) // CHUNK)
    gvb = gv.astype(BF16)
    for h in range(A_HEADS):
        wm = jnp.where(same_batch & causal, wt_ref[h], 0.0).astype(BF16)
        cols = slice(h * A_HEAD_DIM, (h + 1) * A_HEAD_DIM)
        s = jnp.dot(wm, gvb[:, cols], preferred_element_type=F32) + bs_ref[h]
        mix_scr[:, cols] = u[:, cols] * s

    qb = q.astype(BF16)
    qmasks = _head_lane_masks(s_len)
    vmasks = _head_lane_masks(KEY_SPAN)
    allowed = _lane_iota((s_len, KEY_SPAN)) < WINDOW + s_len
    pad = jnp.zeros((KEY_SPAN - WINDOW - s_len, LANES), F32)
    for b in range(n_batch):
        rows = slice(b * s_len, (b + 1) * s_len)
        k_all = jnp.concatenate([ck_ref[b], k[rows], pad], 0)
        v_all = jnp.concatenate([cv_ref[b], v[rows], pad], 0)
        kreps = [a.astype(BF16) for a in _replicate_kv_heads(k_all)]
        vreps = [a.astype(BF16) for a in _replicate_kv_heads(v_all)]
        for g in range(B_KV_HEADS):
            sinks = [sinks_ref[g * GQA_GROUP + h] for h in range(GQA_GROUP)]
            qcols = slice(g * Q_GROUP_WIDTH, (g + 1) * Q_GROUP_WIDTH)
            out = _attn_block(qb[rows, qcols], kreps[g], vreps[g], allowed, sinks, qmasks, vmasks)
            mix_scr[rows, A_WIDTH + g * Q_GROUP_WIDTH:A_WIDTH + (g + 1) * Q_GROUP_WIDTH] = out

    x1 = _merge(x, mix_scr[...], nag_ref[...], nbg_ref[...], wout_ref, ln1g_ref[...], ln1b_ref[...])
    x1b = x1.astype(BF16)

    pos = _row_iota((rows_all, FF_CHUNK)) % s_len
    for c in range(N_FF_CHUNKS):
        cs = slice(c * FF_CHUNK, (c + 1) * FF_CHUNK)
        a = jnp.dot(x1b, wg_ref[:, cs], preferred_element_type=F32)
        up = jnp.dot(x1b, wu_ref[:, cs], preferred_element_type=F32)
        a_ref[:, cs] = a
        a1 = jnp.where(pos < 1, s1_ref[:, cs], pltpu.roll(a, 1, 0))
        a2 = jnp.where(pos < 2, s2_ref[:, cs], pltpu.roll(a, 2, 0))
        cc = a2 * cw_ref[0:1, cs] + a1 * cw_ref[1:2, cs] + a * cw_ref[2:3, cs] + cb_ref[:, cs]
        h_scr[:, cs] = (_gelu(cc) * up).astype(BF16)
    f = jnp.dot(h_scr[...], wd_ref[:, :D_MODEL], preferred_element_type=F32)
    y_ref[...] = _post_norm(x1 + f, ln2g_ref[...], ln2b_ref[...])


def _rope_tables(pos):
    inv = ROPE_THETA ** (-np.arange(HALF, dtype=np.float64) / HALF)
    ang = pos.astype(np.float64)[:, None] * inv[None, :]
    cos, sin = np.cos(ang), np.sin(ang)
    reps = LANES // HEAD_DIM
    return (np.tile(np.concatenate([cos, cos], -1), (1, reps)).astype(np.float32),
            np.tile(np.concatenate([-sin, sin], -1), (1, reps)).astype(np.float32))


def _resident(shape):
    return pl.BlockSpec(shape, lambda *_: (0,) * len(shape), pipeline_mode=pl.Buffered(1))


def kernel(x_prompt, x_sample, cache_k, cache_v, state_ffn_conv, w_in, gmlp_ln_g, gmlp_ln_b,
           gmlp_w_s, gmlp_b_s, attn_sinks, norm_a_g, norm_b_g, w_out, ln1_g, ln1_b,
           w_gate, w_up, conv_w, conv_b, w_down, ln2_g, ln2_b):
    assert w_in.shape[0] == DEPTH == 1
    bp, tp, _ = x_prompt.shape
    bs, ts, _ = x_sample.shape
    tile = SEQ_TILE
    step_rows = TILES_PER_STEP * tile
    assert tp % step_rows == 0 and tile % ROW_BLOCK == 0 and tile >= WINDOW
    assert bs * ts == ROW_BLOCK and WINDOW + ts <= KEY_SPAN and ts >= CONV_WIDTH - 1

    row = lambda a: a[0].reshape(1, -1)
    pad_lanes = lambda w: jnp.pad(w, ((0, 0), (0, LANES)))
    q_cols = (jnp.arange(w_in.shape[-1]) >= 2 * A_WIDTH) & (jnp.arange(w_in.shape[-1]) < 2 * A_WIDTH + B_WIDTH)
    win_b = (w_in[0] * jnp.where(q_cols, QK_SCALE, 1.0)[None, :]).astype(BF16)
    wout_b = pad_lanes((w_out[0] * (1.0 / ALPHA)).astype(BF16))
    wg_b, wu_b = w_gate[0].astype(BF16), w_up[0].astype(BF16)
    wd_b = pad_lanes((w_down[0] * (1.0 / ALPHA)).astype(BF16))
    sinks = attn_sinks[0]
    vec_args = dict(lng=row(gmlp_ln_g), lnb=row(gmlp_ln_b), nag=row(norm_a_g), nbg=row(norm_b_g),
                    ln1g=row(ln1_g), ln1b=row(ln1_b), cb=row(conv_b), ln2g=row(ln2_g), ln2b=row(ln2_b))
    cw = conv_w[0]
    smem = pl.BlockSpec(memory_space=pltpu.SMEM)

    def weight_specs(ws_shape):
        return [_resident(win_b.shape), _resident((1, A_WIDTH)), _resident((1, A_WIDTH)),
                _resident(ws_shape), _resident(ws_shape),
                _resident((1, A_WIDTH)), _resident((1, B_WIDTH)), _resident(wout_b.shape),
                _resident((1, D_MODEL)), _resident((1, D_MODEL)),
                _resident(wg_b.shape), _resident(wu_b.shape), _resident(cw.shape), _resident((1, D_FF)),
                _resident(wd_b.shape), _resident((1, D_MODEL)), _resident((1, D_MODEL))]

    def weight_args(ws, bsb):
        return (win_b, vec_args['lng'], vec_args['lnb'], ws, bsb, vec_args['nag'], vec_args['nbg'],
                wout_b, vec_args['ln1g'], vec_args['ln1b'], wg_b, wu_b, cw, vec_args['cb'], wd_b,
                vec_args['ln2g'], vec_args['ln2b'])

    cos_p, sin_p = _rope_tables(np.arange(tp))
    bias_p = jnp.broadcast_to(gmlp_b_s[0][:, :, None], (A_HEADS, GMLP_CHUNK, A_HEAD_DIM))
    seq_tiles = tp // tile
    seq_blocks = tp // step_rows
    n_blocks = bp * seq_blocks
    n_steps = n_blocks + 1
    def stage_bt(lag):
        def bt(s):
            i = jnp.clip(s - lag, 0, n_blocks - 1)
            return i // seq_blocks, i % seq_blocks
        return bt
    mixer_bt, ffn_bt = stage_bt(0), stage_bt(1)
    y_p, pk, pv, pc = pl.pallas_call(
        functools.partial(_prompt_kernel, tile=tile, seq_tiles=seq_tiles, n_steps=n_steps),
        grid=(n_steps,),
        in_specs=[smem,
                  pl.BlockSpec((1, step_rows, D_MODEL), lambda s: (*mixer_bt(s), 0)),
                  _resident(cos_p.shape), _resident(sin_p.shape)]
                 + weight_specs((A_HEADS, GMLP_CHUNK, GMLP_CHUNK)),
        out_specs=[pl.BlockSpec((1, step_rows, D_MODEL), lambda s: (*ffn_bt(s), 0)),
                   pl.BlockSpec((1, WINDOW, KV_WIDTH), lambda s: (mixer_bt(s)[0], 0, 0)),
                   pl.BlockSpec((1, WINDOW, KV_WIDTH), lambda s: (mixer_bt(s)[0], 0, 0)),
                   pl.BlockSpec((1, SUBLANES, D_FF), lambda s: (ffn_bt(s)[0], 0, 0))],
        out_shape=[jax.ShapeDtypeStruct((bp, tp, D_MODEL), F32),
                   jax.ShapeDtypeStruct((bp, WINDOW, KV_WIDTH), F32),
                   jax.ShapeDtypeStruct((bp, WINDOW, KV_WIDTH), F32),
                   jax.ShapeDtypeStruct((bp, SUBLANES, D_FF), F32)],
        scratch_shapes=[pltpu.VMEM((4, WINDOW, LANES), BF16),
                        pltpu.VMEM((SUBLANES, D_FF), F32),
                        pltpu.VMEM((tile, MIX_WIDTH), F32),
                        pltpu.VMEM((tile, D_FF), BF16),
                        pltpu.VMEM((TILES_PER_STEP, tile, D_MODEL), F32),
                        pltpu.VMEM((tile, D_MODEL), F32),
                        pltpu.VMEM((tile, D_MODEL + LANES), BF16),
                        pltpu.VMEM((tile, D_MODEL + LANES), BF16),
                        pltpu.VMEM((tile, MIX_WIDTH + LANES), BF16)],
        compiler_params=pltpu.CompilerParams(dimension_semantics=("arbitrary",),
                                             vmem_limit_bytes=V7X_VMEM_LIMIT_BYTES),
        name="prompt_layer",
    )(sinks, x_prompt, cos_p, sin_p, *weight_args(gmlp_w_s[0], bias_p))

    rows_all = bs * ts
    cos_s, sin_s = _rope_tables(PAST_LEN + np.arange(ts))
    cos_s, sin_s = np.tile(cos_s, (bs, 1)), np.tile(sin_s, (bs, 1))
    ck = cache_k[0].reshape(bs, WINDOW, KV_WIDTH)
    cv = cache_v[0].reshape(bs, WINDOW, KV_WIDTH)
    st = state_ffn_conv[0]
    pad_frames = lambda a: jnp.pad(a, ((0, 0), (0, ts - a.shape[1]), (0, 0))).reshape(rows_all, D_FF)
    s1, s2 = pad_frames(st[:, 1:]), pad_frames(st)
    w_tiled = jnp.tile(gmlp_w_s[0][:, :ts, :ts], (1, bs, bs))
    bias_s = jnp.broadcast_to(jnp.tile(gmlp_b_s[0][:, :ts], (1, bs))[:, :, None],
                              (A_HEADS, rows_all, A_HEAD_DIM))
    full = lambda shape: pl.BlockSpec(shape, lambda i: (0,) * len(shape))
    y_s, k_s, v_s, gv_s, a_s = pl.pallas_call(
        functools.partial(_sample_kernel, n_batch=bs, s_len=ts),
        grid=(1,),
        in_specs=[smem, _resident((rows_all, D_MODEL)), _resident(cos_s.shape), _resident(sin_s.shape),
                  _resident(ck.shape), _resident(cv.shape), _resident(s1.shape), _resident(s2.shape)]
                 + weight_specs((A_HEADS, rows_all, rows_all)),
        out_specs=[full((rows_all, D_MODEL)), full((rows_all, KV_WIDTH)), full((rows_all, KV_WIDTH)),
                   full((rows_all, A_WIDTH)), full((rows_all, D_FF))],
        out_shape=[jax.ShapeDtypeStruct((rows_all, D_MODEL), F32),
                   jax.ShapeDtypeStruct((rows_all, KV_WIDTH), F32),
                   jax.ShapeDtypeStruct((rows_all, KV_WIDTH), F32),
                   jax.ShapeDtypeStruct((rows_all, A_WIDTH), F32),
                   jax.ShapeDtypeStruct((rows_all, D_FF), F32)],
        scratch_shapes=[pltpu.VMEM((rows_all, MIX_WIDTH), F32),
                        pltpu.VMEM((rows_all, D_FF), BF16)],
        compiler_params=pltpu.CompilerParams(dimension_semantics=("arbitrary",),
                                             vmem_limit_bytes=V7X_VMEM_LIMIT_BYTES),
        name="sample_layer",
    )(sinks, x_sample.reshape(rows_all, D_MODEL), cos_s, sin_s, ck, cv, s1, s2,
      *weight_args(w_tiled, bias_s))

    kv5 = lambda a, n, t: a.reshape(1, n, t, B_KV_HEADS, HEAD_DIM)
    return (y_p, y_s.reshape(bs, ts, D_MODEL),
            kv5(pk, bp, WINDOW), kv5(pv, bp, WINDOW),
            pc[None, :, SUBLANES - (CONV_WIDTH - 1):],
            kv5(k_s, bs, ts), kv5(v_s, bs, ts),
            gv_s.reshape(1, bs, ts, A_HEADS, A_HEAD_DIM),
            a_s.reshape(bs, ts, D_FF)[None, :, ts - (CONV_WIDTH - 1):])
```

```python
import functools
import math

import jax
import jax.numpy as jnp
import numpy as np
from jax import lax
from jax.experimental import pallas as pl
from jax.experimental.pallas import tpu as pltpu

D_MODEL = 1024
CHUNK = 64
HEAD_DIM = 64
HALF = HEAD_DIM // 2
A_HEADS = 4
A_HEAD_DIM = 128
A_WIDTH = A_HEADS * A_HEAD_DIM
GMLP_CHUNK = 128
B_HEADS = 8
B_KV_HEADS = 2
GQA_GROUP = B_HEADS // B_KV_HEADS
B_WIDTH = B_HEADS * HEAD_DIM
KV_WIDTH = B_KV_HEADS * HEAD_DIM
WINDOW = 128
MIX_WIDTH = A_WIDTH + B_WIDTH
D_FF = 2816
CONV_WIDTH = 3
PAST_LEN = 2048
ROPE_THETA = 10000.0
LN_EPS = 1e-5
RMS_EPS = 1e-6
DEPTH = 1
ALPHA = (2 * DEPTH) ** 0.25
ATTN_SCALE = HEAD_DIM ** -0.5
LOG2_E = math.log2(math.e)
QK_SCALE = ATTN_SCALE * LOG2_E

LANES = 128
SUBLANES = 8
V7X_VMEM_LIMIT_BYTES = 58 * 1024 * 1024

Q_GROUP_WIDTH = GQA_GROUP * HEAD_DIM
KEY_SPAN = 2 * WINDOW
ROW_BLOCK = 128
FF_CHUNK = 256
N_FF_CHUNKS = D_FF // FF_CHUNK
PROJ_BLOCK = 256
DOWN_BLOCK = 256
WOUT_BLOCK = 512
SEQ_TILE = 512
ATTN_WAVE = 4
TILES_PER_STEP = 1

F32 = jnp.float32
BF16 = jnp.bfloat16


def _gelu(x):
    k0 = -2.0 * math.sqrt(2.0 / math.pi) * math.log2(math.e)
    k1 = k0 * 0.044715
    return x * (1.0 / (1.0 + jnp.exp2(x * (k0 + k1 * (x * x)))))


def _layer_norm(x, g, b, eps=LN_EPS):
    mu = jnp.mean(x, -1, keepdims=True)
    xc = x - mu
    var = jnp.mean(xc * xc, -1, keepdims=True)
    return xc * lax.rsqrt(var + eps) * g + b


def _post_norm(v, g, b):
    return _layer_norm(v, g, b, eps=LN_EPS / (ALPHA * ALPHA))


def _rms_norm(x, g):
    ms = jnp.mean(x * x, -1, keepdims=True)
    return x * lax.rsqrt(ms + RMS_EPS) * g


def _lane_iota(shape):
    return lax.broadcasted_iota(jnp.int32, shape, 1)


def _row_iota(shape):
    return lax.broadcasted_iota(jnp.int32, shape, 0)


def _rope(x, cos, sin_signed):
    lo = (_lane_iota(cos.shape) % HEAD_DIM) < HALF
    outs = []
    for i in range(x.shape[1] // LANES):
        xs = x[:, i * LANES:(i + 1) * LANES]
        partner = jnp.where(lo, pltpu.roll(xs, LANES - HALF, 1), pltpu.roll(xs, HALF, 1))
        outs.append(xs * cos + partner * sin_signed)
    return outs[0] if len(outs) == 1 else jnp.concatenate(outs, 1)


def _replicate_kv_heads(x):
    lo = _lane_iota(x.shape) < HEAD_DIM
    sw = pltpu.roll(x, HEAD_DIM, 1)
    return jnp.where(lo, x, sw), jnp.where(lo, sw, x)


def _head_lane_masks(rows):
    lane = _lane_iota((rows, Q_GROUP_WIDTH))
    return [jnp.where((lane >= h * HEAD_DIM) & (lane < (h + 1) * HEAD_DIM), 1.0, 0.0).astype(BF16)
            for h in range(GQA_GROUP)]


def _attn_scores(qg, krep, qmasks):
    k2 = jnp.concatenate([krep, krep], 1)
    qm = jnp.concatenate([qg * qmasks[h] for h in range(GQA_GROUP)], 0)
    return lax.dot_general(qm, k2, (((1,), (1,)), ((), ())), preferred_element_type=F32)


def _attn_probs(sc, allowed, sinks):
    r = sc.shape[0] // GQA_GROUP
    ps = []
    for h in range(GQA_GROUP):
        s = jnp.where(allowed, sc[h * r:(h + 1) * r], -jnp.inf)
        sink = sinks[h] * LOG2_E
        m = jnp.maximum(jnp.max(s, -1, keepdims=True), sink)
        e = jnp.exp2(s - m)
        den = jnp.sum(e, -1, keepdims=True) + jnp.exp2(sink - m)
        ps.append((e * (1.0 / den)).astype(BF16))
    return jnp.concatenate(ps, 1)


def _attn_values(pc, vrep, vmasks):
    v2 = jnp.concatenate([vrep, vrep], 1)
    vm = jnp.concatenate([v2 * vmasks[h] for h in range(GQA_GROUP)], 0)
    return jnp.dot(pc, vm, preferred_element_type=F32)


def _attn_block(qg, krep, vrep, allowed, sinks, qmasks, vmasks):
    return _attn_values(_attn_probs(_attn_scores(qg, krep, qmasks), allowed, sinks), vrep, vmasks)


def _mixer_inputs(xb, win_ref, lng, lnb, cos, sin_signed):
    za = jnp.dot(xb, win_ref[:, :2 * A_WIDTH], preferred_element_type=F32)
    u = _gelu(za[:, :A_WIDTH])
    gv = _layer_norm(_gelu(za[:, A_WIDTH:]), lng, lnb)
    zb = jnp.dot(xb, win_ref[:, 2 * A_WIDTH:], preferred_element_type=F32)
    qs = _rope(zb[:, :B_WIDTH], cos, sin_signed)
    k = _rope(zb[:, B_WIDTH:B_WIDTH + KV_WIDTH], cos, sin_signed)
    v = zb[:, B_WIDTH + KV_WIDTH:]
    return u, gv, qs, k, v


def _merge(x, mix, nag, nbg, wout_ref, ln1g, ln1b):
    mi = jnp.concatenate([_rms_norm(mix[:, :A_WIDTH], nag), _rms_norm(mix[:, A_WIDTH:], nbg)], 1)
    m = jnp.dot(mi.astype(BF16), wout_ref[:, :D_MODEL], preferred_element_type=F32)
    return _post_norm(x + m, ln1g, ln1b)


def _conv_taps(a, prev8):
    r1 = pltpu.roll(a, 1, 0)
    r2 = pltpu.roll(a, 2, 0)
    row = _row_iota(prev8.shape)
    first1 = jnp.where(row < 1, pltpu.roll(prev8, 1, 0), r1[:SUBLANES])
    first2 = jnp.where(row < 2, pltpu.roll(prev8, 2, 0), r2[:SUBLANES])
    return (jnp.concatenate([first1, r1[SUBLANES:]], 0), jnp.concatenate([first2, r2[SUBLANES:]], 0))


def _prompt_kernel(sinks_ref, x_ref, cos_ref, sin_ref, win_ref, lng_ref, lnb_ref, ws_ref, bs_ref,
                   nag_ref, nbg_ref, wout_ref, ln1g_ref, ln1b_ref, wg_ref, wu_ref, cw_ref, cb_ref,
                   wd_ref, ln2g_ref, ln2b_ref,
                   y_ref, pk_ref, pv_ref, pc_ref,
                   kv_scr, conv_scr, mix_scr, h_scr, pre1_scr, x1c_scr, x1b_scr, xb_scr, mib_scr,
                   *, tile, seq_tiles, n_steps):
    s = pl.program_id(0)
    mixer_tile0 = TILES_PER_STEP * jnp.minimum(s, n_steps - 2)
    ffn_tile0 = TILES_PER_STEP * jnp.maximum(s - 1, 0)

    @pl.when(s == 0)
    def _():
        pre1_scr[...] = jnp.zeros_like(pre1_scr)

    @pl.when(mixer_tile0 % seq_tiles == 0)
    def _():
        kv_scr[...] = jnp.zeros_like(kv_scr)

    @pl.when(ffn_tile0 % seq_tiles == 0)
    def _():
        conv_scr[...] = jnp.zeros_like(conv_scr)

    row_blocks = [slice(j * ROW_BLOCK, (j + 1) * ROW_BLOCK) for j in range(tile // ROW_BLOCK)]
    attn_blocks = [(j, g) for j in range(tile // ROW_BLOCK) for g in range(B_KV_HEADS)]
    n_proj = win_ref.shape[1] // PROJ_BLOCK
    n_rb, n_ab = len(row_blocks), len(attn_blocks)
    assert n_proj == 7 and n_rb <= 5 and n_ab <= 2 * ATTN_WAVE and N_FF_CHUNKS == 11
    assert D_MODEL // DOWN_BLOCK == 4 and D_MODEL // WOUT_BLOCK == 2

    def pipeline_slot(k):
        tm = (mixer_tile0 + k) % seq_tiles
        trows = slice(k * tile, (k + 1) * tile)
        pre1 = pre1_scr.at[k]
        st = {}

        def proj_cast():
            xb_scr[...] = x_ref[0, trows, :].astype(BF16)

        def proj_dot(i):
            cols = slice(i * PROJ_BLOCK, (i + 1) * PROJ_BLOCK)
            st['z', i] = jnp.dot(xb_scr[...], win_ref[:, cols], preferred_element_type=F32)

        def epi_u(i):
            st['u', i] = _gelu(st.pop(('z', i)))

        def epi_gv():
            g = jnp.concatenate([_gelu(st.pop(('z', 2))), _gelu(st.pop(('z', 3)))], 1)
            st['gvb'] = _layer_norm(g, lng_ref[...], lnb_ref[...]).astype(BF16)

        def rope_tables():
            row0 = pl.multiple_of(tm * tile, tile)
            return cos_ref[pl.ds(row0, tile), :], sin_ref[pl.ds(row0, tile), :]

        def epi_q(i):
            cos, sin_signed = rope_tables()
            st['qb', i - 4] = _rope(st.pop(('z', i)), cos, sin_signed).astype(BF16)

        def epi_kv():
            cos, sin_signed = rope_tables()
            zkv = st.pop(('z', 6))
            kk = _rope(zkv[:, :KV_WIDTH], cos, sin_signed)
            vv = zkv[:, KV_WIDTH:]
            pk_ref[0] = kk[tile - WINDOW:]
            pv_ref[0] = vv[tile - WINDOW:]
            k0, k1 = _replicate_kv_heads(kk)
            v0, v1 = _replicate_kv_heads(vv)
            reps = [jnp.concatenate([kv_scr[i], a.astype(BF16)], 0) for i, a in enumerate((k0, k1, v0, v1))]
            for i in range(4):
                kv_scr[i] = reps[i][tile:]
            st['reps'] = reps

        def gmlp():
            cidx_r = _row_iota((GMLP_CHUNK, GMLP_CHUNK)) // CHUNK
            cidx_c = _lane_iota((GMLP_CHUNK, GMLP_CHUNK)) // CHUNK
            gvb = st.pop('gvb')
            zero = jnp.zeros((GMLP_CHUNK, A_HEAD_DIM), BF16)
            for hp in range(A_HEADS // 2):
                ha, hb = 2 * hp, 2 * hp + 1
                wm = jnp.concatenate([jnp.where(cidx_r >= cidx_c, ws_ref[h], 0.0) for h in (ha, hb)],
                                     1).astype(BF16)
                bias = jnp.concatenate([bs_ref[ha], bs_ref[hb]], 1)
                cols = slice(ha * A_HEAD_DIM, (hb + 1) * A_HEAD_DIM)
                u = st.pop(('u', hp))
                for c in range(tile // GMLP_CHUNK):
                    rows = slice(c * GMLP_CHUNK, (c + 1) * GMLP_CHUNK)
                    ga = gvb[rows, ha * A_HEAD_DIM:(ha + 1) * A_HEAD_DIM]
                    gb = gvb[rows, hb * A_HEAD_DIM:(hb + 1) * A_HEAD_DIM]
                    rhs = jnp.concatenate([jnp.concatenate([ga, zero], 1), jnp.concatenate([zero, gb], 1)], 0)
                    sg = jnp.dot(wm, rhs, preferred_element_type=F32) + bias
                    mix_scr[rows, cols] = u[rows, :] * sg

        def attn_prep():
            qc = _row_iota((ROW_BLOCK, KEY_SPAN)) // CHUNK
            kc = _lane_iota((ROW_BLOCK, KEY_SPAN)) // CHUNK
            in_band = (kc >= qc) & (kc <= qc + WINDOW // CHUNK)
            first_lo = jnp.where(tm > 0, 0, WINDOW // CHUNK)
            st.update(in_band=in_band, allowed_first=in_band & (kc >= first_lo),
                      qmasks=_head_lane_masks(ROW_BLOCK), vmasks=_head_lane_masks(KEY_SPAN))

        def attn_qk(i):
            j, g = attn_blocks[i]
            qg = st['qb', g][row_blocks[j], :]
            st['sc', i] = _attn_scores(qg, st['reps'][g][j * ROW_BLOCK:j * ROW_BLOCK + KEY_SPAN],
                                       st['qmasks'])

        def attn_softmax(i):
            j, g = attn_blocks[i]
            sinks = [sinks_ref[g * GQA_GROUP + h] for h in range(GQA_GROUP)]
            allowed = st['allowed_first'] if j == 0 else st['in_band']
            st['p', i] = _attn_probs(st.pop(('sc', i)), allowed, sinks)

        def attn_pv(i):
            j, g = attn_blocks[i]
            out = _attn_values(st.pop(('p', i)),
                               st['reps'][2 + g][j * ROW_BLOCK:j * ROW_BLOCK + KEY_SPAN], st['vmasks'])
            mix_scr[row_blocks[j], A_WIDTH + g * Q_GROUP_WIDTH:A_WIDTH + (g + 1) * Q_GROUP_WIDTH] = out

        def rms(j):
            rows = row_blocks[j]
            mi = jnp.concatenate([_rms_norm(mix_scr[rows, :A_WIDTH], nag_ref[...]),
                                  _rms_norm(mix_scr[rows, A_WIDTH:], nbg_ref[...])], 1)
            mib_scr[rows, :] = mi.astype(BF16)

        def wout(nb):
            cols = slice(nb * WOUT_BLOCK, (nb + 1) * WOUT_BLOCK)
            m = jnp.dot(mib_scr[...], wout_ref[:, cols], preferred_element_type=F32)
            pre1[:, cols] = x_ref[0, trows, cols] + m

        def ln1(j):
            rows = row_blocks[j]
            x1 = _post_norm(pre1[rows, :], ln1g_ref[...], ln1b_ref[...])
            x1c_scr[rows, :] = x1
            x1b_scr[rows, :] = x1.astype(BF16)

        def ffn_dots(c):
            cs = slice(c * FF_CHUNK, (c + 1) * FF_CHUNK)
            x1b = x1b_scr[...]
            st['a', c] = jnp.dot(x1b, wg_ref[:, cs], preferred_element_type=F32)
            st['up', c] = jnp.dot(x1b, wu_ref[:, cs], preferred_element_type=F32)

        def ffn_epi(c):
            cs = slice(c * FF_CHUNK, (c + 1) * FF_CHUNK)
            a, up = st.pop(('a', c)), st.pop(('up', c))
            a1, a2 = _conv_taps(a, conv_scr[:, cs])
            cc = a2 * cw_ref[0:1, cs] + a1 * cw_ref[1:2, cs] + a * cw_ref[2:3, cs] + cb_ref[:, cs]
            h_scr[:, cs] = (_gelu(cc) * up).astype(BF16)
            conv_scr[:, cs] = a[tile - SUBLANES:]

        def down(nb):
            cols = slice(nb * DOWN_BLOCK, (nb + 1) * DOWN_BLOCK)
            f = jnp.dot(h_scr[...], wd_ref[:, cols], preferred_element_type=F32)
            st['pre2', nb] = x1c_scr[:, cols] + f

        def ln2():
            pre2 = jnp.concatenate([st.pop(('pre2', nb)) for nb in range(D_MODEL // DOWN_BLOCK)], 1)
            y_ref[0, trows, :] = _post_norm(pre2, ln2g_ref[...], ln2b_ref[...])

        proj_cast()
        for i in range(5):
            proj_dot(i)
            if i < n_rb:
                ln1(i)
        epi_u(0)
        ffn_dots(0)
        epi_u(1)
        ffn_dots(1)
        ffn_epi(0)
        proj_dot(5)
        ffn_dots(2)
        ffn_epi(1)
        epi_gv()
        proj_dot(6)
        ffn_dots(3)
        ffn_epi(2)
        epi_q(4)
        ffn_dots(4)
        ffn_epi(3)
        gmlp()
        epi_q(5)
        ffn_dots(5)
        ffn_epi(4)
        epi_kv()
        attn_prep()
        for c in range(6, N_FF_CHUNKS):
            ffn_dots(c)
            ffn_epi(c - 1)
        wave0 = list(range(min(ATTN_WAVE, n_ab)))
        wave1 = list(range(ATTN_WAVE, n_ab))
        for i in wave0[:2]:
            attn_qk(i)
        ffn_epi(N_FF_CHUNKS - 1)
        for i in wave0[2:]:
            attn_qk(i)
        down(0)
        for i in wave0:
            attn_softmax(i)
        for i in wave1:
            attn_qk(i)
        down(1)
        for i in wave0:
            attn_pv(i)
        for i in wave1:
            attn_softmax(i)
        down(2)
        for i in wave1:
            attn_pv(i)
        for j in range(n_rb - 1):
            rms(j)
        down(3)
        rms(n_rb - 1)
        wout(0)
        ln2()
        wout(1)

    for k in range(TILES_PER_STEP):
        pipeline_slot(k)
    pc_ref[0] = conv_scr[...]


def _sample_kernel(sinks_ref, x_ref, cos_ref, sin_ref, ck_ref, cv_ref, s1_ref, s2_ref,
                   win_ref, lng_ref, lnb_ref, wt_ref, bs_ref,
                   nag_ref, nbg_ref, wout_ref, ln1g_ref, ln1b_ref, wg_ref, wu_ref, cw_ref, cb_ref,
                   wd_ref, ln2g_ref, ln2b_ref,
                   y_ref, k_ref, v_ref, gv_ref, a_ref,
                   mix_scr, h_scr, *, n_batch, s_len):
    rows_all = n_batch * s_len
    x = x_ref[...]
    xb = x.astype(BF16)
    u, gv, q, k, v = _mixer_inputs(xb, win_ref, lng_ref[...], lnb_ref[...], cos_ref[...], sin_ref[...])
    gv_ref[...] = gv
    k_ref[...] = k
    v_ref[...] = v

    ri = _row_iota((rows_all, rows_all))
    ci = _lane_iota((rows_all, rows_all))
    same_batch = (ri // s_len) == (ci // s_len)
    causal = ((ri % s_len) // CHUNK) >= ((ci % s_len) // CHUNK)
    gvb = gv.astype(BF16)
    for h in range(A_HEADS):
        wm = jnp.where(same_batch & causal, wt_ref[h], 0.0).astype(BF16)
        cols = slice(h * A_HEAD_DIM, (h + 1) * A_HEAD_DIM)
        s = jnp.dot(wm, gvb[:, cols], preferred_element_type=F32) + bs_ref[h]
        mix_scr[:, cols] = u[:, cols] * s

    qb = q.astype(BF16)
    qmasks = _head_lane_masks(s_len)
    vmasks = _head_lane_masks(KEY_SPAN)
    allowed = _lane_iota((s_len, KEY_SPAN)) < WINDOW + s_len
    pad = jnp.zeros((KEY_SPAN - WINDOW - s_len, LANES), F32)
    for b in range(n_batch):
        rows = slice(b * s_len, (b + 1) * s_len)
        k_all = jnp.concatenate([ck_ref[b], k[rows], pad], 0)
        v_all = jnp.concatenate([cv_ref[b], v[rows], pad], 0)
        kreps = [a.astype(BF16) for a in _replicate_kv_heads(k_all)]
        vreps = [a.astype(BF16) for a in _replicate_kv_heads(v_all)]
        for g in range(B_KV_HEADS):
            sinks = [sinks_ref[g * GQA_GROUP + h] for h in range(GQA_GROUP)]
            qcols = slice(g * Q_GROUP_WIDTH, (g + 1) * Q_GROUP_WIDTH)
            out = _attn_block(qb[rows, qcols], kreps[g], vreps[g], allowed, sinks, qmasks, vmasks)
            mix_scr[rows, A_WIDTH + g * Q_GROUP_WIDTH:A_WIDTH + (g + 1) * Q_GROUP_WIDTH] = out

    x1 = _merge(x, mix_scr[...], nag_ref[...], nbg_ref[...], wout_ref, ln1g_ref[...], ln1b_ref[...])
    x1b = x1.astype(BF16)

    pos = _row_iota((rows_all, FF_CHUNK)) % s_len
    for c in range(N_FF_CHUNKS):
        cs = slice(c * FF_CHUNK, (c + 1) * FF_CHUNK)
        a = jnp.dot(x1b, wg_ref[:, cs], preferred_element_type=F32)
        up = jnp.dot(x1b, wu_ref[:, cs], preferred_element_type=F32)
        a_ref[:, cs] = a
        a1 = jnp.where(pos < 1, s1_ref[:, cs], pltpu.roll(a, 1, 0))
        a2 = jnp.where(pos < 2, s2_ref[:, cs], pltpu.roll(a, 2, 0))
        cc = a2 * cw_ref[0:1, cs] + a1 * cw_ref[1:2, cs] + a * cw_ref[2:3, cs] + cb_ref[:, cs]
        h_scr[:, cs] = (_gelu(cc) * up).astype(BF16)
    f = jnp.dot(h_scr[...], wd_ref[:, :D_MODEL], preferred_element_type=F32)
    y_ref[...] = _post_norm(x1 + f, ln2g_ref[...], ln2b_ref[...])


def _rope_tables(pos):
    inv = ROPE_THETA ** (-np.arange(HALF, dtype=np.float64) / HALF)
    ang = pos.astype(np.float64)[:, None] * inv[None, :]
    cos, sin = np.cos(ang), np.sin(ang)
    reps = LANES // HEAD_DIM
    return (np.tile(np.concatenate([cos, cos], -1), (1, reps)).astype(np.float32),
            np.tile(np.concatenate([-sin, sin], -1), (1, reps)).astype(np.float32))


def _resident(shape):
    return pl.BlockSpec(shape, lambda *_: (0,) * len(shape), pipeline_mode=pl.Buffered(1))


def kernel(x_prompt, x_sample, cache_k, cache_v, state_ffn_conv, w_in, gmlp_ln_g, gmlp_ln_b,
           gmlp_w_s, gmlp_b_s, attn_sinks, norm_a_g, norm_b_g, w_out, ln1_g, ln1_b,
           w_gate, w_up, conv_w, conv_b, w_down, ln2_g, ln2_b):
    assert w_in.shape[0] == DEPTH == 1
    bp, tp, _ = x_prompt.shape
    bs, ts, _ = x_sample.shape
    tile = SEQ_TILE
    step_rows = TILES_PER_STEP * tile
    assert tp % step_rows == 0 and tile % ROW_BLOCK == 0 and tile >= WINDOW
    assert bs * ts == ROW_BLOCK and WINDOW + ts <= KEY_SPAN and ts >= CONV_WIDTH - 1

    row = lambda a: a[0].reshape(1, -1)
    pad_lanes = lambda w: jnp.pad(w, ((0, 0), (0, LANES)))
    q_cols = (jnp.arange(w_in.shape[-1]) >= 2 * A_WIDTH) & (jnp.arange(w_in.shape[-1]) < 2 * A_WIDTH + B_WIDTH)
    win_b = (w_in[0] * jnp.where(q_cols, QK_SCALE, 1.0)[None, :]).astype(BF16)
    wout_b = pad_lanes((w_out[0] * (1.0 / ALPHA)).astype(BF16))
    wg_b, wu_b = w_gate[0].astype(BF16), w_up[0].astype(BF16)
    wd_b = pad_lanes((w_down[0] * (1.0 / ALPHA)).astype(BF16))
    sinks = attn_sinks[0]
    vec_args = dict(lng=row(gmlp_ln_g), lnb=row(gmlp_ln_b), nag=row(norm_a_g), nbg=row(norm_b_g),
                    ln1g=row(ln1_g), ln1b=row(ln1_b), cb=row(conv_b), ln2g=row(ln2_g), ln2b=row(ln2_b))
    cw = conv_w[0]
    smem = pl.BlockSpec(memory_space=pltpu.SMEM)

    def weight_specs(ws_shape):
        return [_resident(win_b.shape), _resident((1, A_WIDTH)), _resident((1, A_WIDTH)),
                _resident(ws_shape), _resident(ws_shape),
                _resident((1, A_WIDTH)), _resident((1, B_WIDTH)), _resident(wout_b.shape),
                _resident((1, D_MODEL)), _resident((1, D_MODEL)),
                _resident(wg_b.shape), _resident(wu_b.shape), _resident(cw.shape), _resident((1, D_FF)),
                _resident(wd_b.shape), _resident((1, D_MODEL)), _resident((1, D_MODEL))]

    def weight_args(ws, bsb):
        return (win_b, vec_args['lng'], vec_args['lnb'], ws, bsb, vec_args['nag'], vec_args['nbg'],
                wout_b, vec_args['ln1g'], vec_args['ln1b'], wg_b, wu_b, cw, vec_args['cb'], wd_b,
                vec_args['ln2g'], vec_args['ln2b'])

    cos_p, sin_p = _rope_tables(np.arange(tp))
    bias_p = jnp.broadcast_to(gmlp_b_s[0][:, :, None], (A_HEADS, GMLP_CHUNK, A_HEAD_DIM))
    seq_tiles = tp // tile
    seq_blocks = tp // step_rows
    n_blocks = bp * seq_blocks
    n_steps = n_blocks + 1
    def stage_bt(lag):
        def bt(s):
            i = jnp.clip(s - lag, 0, n_blocks - 1)
            return i // seq_blocks, i % seq_blocks
        return bt
    mixer_bt, ffn_bt = stage_bt(0), stage_bt(1)
    y_p, pk, pv, pc = pl.pallas_call(
        functools.partial(_prompt_kernel, tile=tile, seq_tiles=seq_tiles, n_steps=n_steps),
        grid=(n_steps,),
        in_specs=[smem,
                  pl.BlockSpec((1, step_rows, D_MODEL), lambda s: (*mixer_bt(s), 0)),
                  _resident(cos_p.shape), _resident(sin_p.shape)]
                 + weight_specs((A_HEADS, GMLP_CHUNK, GMLP_CHUNK)),
        out_specs=[pl.BlockSpec((1, step_rows, D_MODEL), lambda s: (*ffn_bt(s), 0)),
                   pl.BlockSpec((1, WINDOW, KV_WIDTH), lambda s: (mixer_bt(s)[0], 0, 0)),
                   pl.BlockSpec((1, WINDOW, KV_WIDTH), lambda s: (mixer_bt(s)[0], 0, 0)),
                   pl.BlockSpec((1, SUBLANES, D_FF), lambda s: (ffn_bt(s)[0], 0, 0))],
        out_shape=[jax.ShapeDtypeStruct((bp, tp, D_MODEL), F32),
                   jax.ShapeDtypeStruct((bp, WINDOW, KV_WIDTH), F32),
                   jax.ShapeDtypeStruct((bp, WINDOW, KV_WIDTH), F32),
                   jax.ShapeDtypeStruct((bp, SUBLANES, D_FF), F32)],
        scratch_shapes=[pltpu.VMEM((4, WINDOW, LANES), BF16),
                        pltpu.VMEM((SUBLANES, D_FF), F32),
                        pltpu.VMEM((tile, MIX_WIDTH), F32),
                        pltpu.VMEM((tile, D_FF), BF16),
                        pltpu.VMEM((TILES_PER_STEP, tile, D_MODEL), F32),
                        pltpu.VMEM((tile, D_MODEL), F32),
                        pltpu.VMEM((tile, D_MODEL), BF16),
                        pltpu.VMEM((tile, D_MODEL), BF16),
                        pltpu.VMEM((tile, MIX_WIDTH), BF16)],
        compiler_params=pltpu.CompilerParams(dimension_semantics=("arbitrary",),
                                             vmem_limit_bytes=V7X_VMEM_LIMIT_BYTES),
        name="prompt_layer",
    )(sinks, x_prompt, cos_p, sin_p, *weight_args(gmlp_w_s[0], bias_p))

    rows_all = bs * ts
    cos_s, sin_s = _rope_tables(PAST_LEN + np.arange(ts))
    cos_s, sin_s = np.tile(cos_s, (bs, 1)), np.tile(sin_s, (bs, 1))
    ck = cache_k[0].reshape(bs, WINDOW, KV_WIDTH)
    cv = cache_v[0].reshape(bs, WINDOW, KV_WIDTH)
    st = state_ffn_conv[0]
    pad_frames = lambda a: jnp.pad(a, ((0, 0), (0, ts - a.shape[1]), (0, 0))).reshape(rows_all, D_FF)
    s1, s2 = pad_frames(st[:, 1:]), pad_frames(st)
    w_tiled = jnp.tile(gmlp_w_s[0][:, :ts, :ts], (1, bs, bs))
    bias_s = jnp.broadcast_to(jnp.tile(gmlp_b_s[0][:, :ts], (1, bs))[:, :, None],
                              (A_HEADS, rows_all, A_HEAD_DIM))
    full = lambda shape: pl.BlockSpec(shape, lambda i: (0,) * len(shape))
    y_s, k_s, v_s, gv_s, a_s = pl.pallas_call(
        functools.partial(_sample_kernel, n_batch=bs, s_len=ts),
        grid=(1,),
        in_specs=[smem, _resident((rows_all, D_MODEL)), _resident(cos_s.shape), _resident(sin_s.shape),
                  _resident(ck.shape), _resident(cv.shape), _resident(s1.shape), _resident(s2.shape)]
                 + weight_specs((A_HEADS, rows_all, rows_all)),
        out_specs=[full((rows_all, D_MODEL)), full((rows_all, KV_WIDTH)), full((rows_all, KV_WIDTH)),
                   full((rows_all, A_WIDTH)), full((rows_all, D_FF))],
        out_shape=[jax.ShapeDtypeStruct((rows_all, D_MODEL), F32),
                   jax.ShapeDtypeStruct((rows_all, KV_WIDTH), F32),
                   jax.ShapeDtypeStruct((rows_all, KV_WIDTH), F32),
                   jax.ShapeDtypeStruct((rows_all, A_WIDTH), F32),
                   jax.ShapeDtypeStruct((rows_all, D_FF), F32)],
        scratch_shapes=[pltpu.VMEM((rows_all, MIX_WIDTH), F32),
                        pltpu.VMEM((rows_all, D_FF), BF16)],
        compiler_params=pltpu.CompilerParams(dimension_semantics=("arbitrary",),
                                             vmem_limit_bytes=V7X_VMEM_LIMIT_BYTES),
        name="sample_layer",
    )(sinks, x_sample.reshape(rows_all, D_MODEL), cos_s, sin_s, ck, cv, s1, s2,
      *weight_args(w_tiled, bias_s))

    kv5 = lambda a, n, t: a.reshape(1, n, t, B_KV_HEADS, HEAD_DIM)
    return (y_p, y_s.reshape(bs, ts, D_MODEL),
            kv5(pk, bp, WINDOW), kv5(pv, bp, WINDOW),
            pc[None, :, SUBLANES - (CONV_WIDTH - 1):],
            kv5(k_s, bs, ts), kv5(v_s, bs, ts),
            gv_s.reshape(1, bs, ts, A_HEADS, A_HEAD_DIM),
            a_s.reshape(bs, ts, D_FF)[None, :, ts - (CONV_WIDTH - 1):])
```

```python
import functools
import math

import jax
import jax.numpy as jnp
import numpy as np
from jax import lax
from jax.experimental import pallas as pl
from jax.experimental.pallas import tpu as pltpu

D_MODEL = 1024
CHUNK = 64
HEAD_DIM = 64
HALF = HEAD_DIM // 2
A_HEADS = 4
A_HEAD_DIM = 128
A_WIDTH = A_HEADS * A_HEAD_DIM
GMLP_CHUNK = 128
B_HEADS = 8
B_KV_HEADS = 2
GQA_GROUP = B_HEADS // B_KV_HEADS
B_WIDTH = B_HEADS * HEAD_DIM
KV_WIDTH = B_KV_HEADS * HEAD_DIM
WINDOW = 128
MIX_WIDTH = A_WIDTH + B_WIDTH
D_FF = 2816
CONV_WIDTH = 3
PAST_LEN = 2048
ROPE_THETA = 10000.0
LN_EPS = 1e-5
RMS_EPS = 1e-6
DEPTH = 1
ALPHA = (2 * DEPTH) ** 0.25
ATTN_SCALE = HEAD_DIM ** -0.5
LOG2_E = math.log2(math.e)
QK_SCALE = ATTN_SCALE * LOG2_E

LANES = 128
SUBLANES = 8
V7X_VMEM_LIMIT_BYTES = 56 * 1024 * 1024

Q_GROUP_WIDTH = GQA_GROUP * HEAD_DIM
KEY_SPAN = 2 * WINDOW
ROW_BLOCK = 128
FF_CHUNK = 256
N_FF_CHUNKS = D_FF // FF_CHUNK
PROJ_BLOCK = 256
DOWN_BLOCK = 256
WOUT_BLOCK = 512
SEQ_TILE = 256
TILES_PER_STEP = 1

F32 = jnp.float32
BF16 = jnp.bfloat16


def _gelu(x):
    k0 = -2.0 * math.sqrt(2.0 / math.pi) * math.log2(math.e)
    k1 = k0 * 0.044715
    return x * (1.0 / (1.0 + jnp.exp2(x * (k0 + k1 * (x * x)))))


def _layer_norm(x, g, b, eps=LN_EPS):
    mu = jnp.mean(x, -1, keepdims=True)
    xc = x - mu
    var = jnp.mean(xc * xc, -1, keepdims=True)
    return xc * lax.rsqrt(var + eps) * g + b


def _post_norm(v, g, b):
    return _layer_norm(v, g, b, eps=LN_EPS / (ALPHA * ALPHA))


def _rms_norm(x, g):
    ms = jnp.mean(x * x, -1, keepdims=True)
    return x * lax.rsqrt(ms + RMS_EPS) * g


def _lane_iota(shape):
    return lax.broadcasted_iota(jnp.int32, shape, 1)


def _row_iota(shape):
    return lax.broadcasted_iota(jnp.int32, shape, 0)


def _rope(x, cos, sin_signed):
    lo = (_lane_iota(cos.shape) % HEAD_DIM) < HALF
    outs = []
    for i in range(x.shape[1] // LANES):
        xs = x[:, i * LANES:(i + 1) * LANES]
        partner = jnp.where(lo, pltpu.roll(xs, LANES - HALF, 1), pltpu.roll(xs, HALF, 1))
        outs.append(xs * cos + partner * sin_signed)
    return outs[0] if len(outs) == 1 else jnp.concatenate(outs, 1)


def _replicate_kv_heads(x):
    lo = _lane_iota(x.shape) < HEAD_DIM
    sw = pltpu.roll(x, HEAD_DIM, 1)
    return jnp.where(lo, x, sw), jnp.where(lo, sw, x)


def _head_lane_masks(rows):
    lane = _lane_iota((rows, Q_GROUP_WIDTH))
    return [jnp.where((lane >= h * HEAD_DIM) & (lane < (h + 1) * HEAD_DIM), 1.0, 0.0).astype(BF16)
            for h in range(GQA_GROUP)]


def _attn_scores(qg, krep, qmasks):
    k2 = jnp.concatenate([krep, krep], 1)
    qm = jnp.concatenate([qg * qmasks[h] for h in range(GQA_GROUP)], 0)
    return lax.dot_general(qm, k2, (((1,), (1,)), ((), ())), preferred_element_type=F32)


def _attn_probs(sc, allowed, sinks):
    r = sc.shape[0] // GQA_GROUP
    ps = []
    for h in range(GQA_GROUP):
        s = jnp.where(allowed, sc[h * r:(h + 1) * r], -jnp.inf)
        sink = sinks[h] * LOG2_E
        m = jnp.maximum(jnp.max(s, -1, keepdims=True), sink)
        e = jnp.exp2(s - m)
        den = jnp.sum(e, -1, keepdims=True) + jnp.exp2(sink - m)
        ps.append((e * (1.0 / den)).astype(BF16))
    return jnp.concatenate(ps, 1)


def _attn_values(pc, vrep, vmasks):
    v2 = jnp.concatenate([vrep, vrep], 1)
    vm = jnp.concatenate([v2 * vmasks[h] for h in range(GQA_GROUP)], 0)
    return jnp.dot(pc, vm, preferred_element_type=F32)


def _attn_block(qg, krep, vrep, allowed, sinks, qmasks, vmasks):
    return _attn_values(_attn_probs(_attn_scores(qg, krep, qmasks), allowed, sinks), vrep, vmasks)


def _mixer_inputs(xb, win_ref, lng, lnb, cos, sin_signed):
    za = jnp.dot(xb, win_ref[:, :2 * A_WIDTH], preferred_element_type=F32)
    u = _gelu(za[:, :A_WIDTH])
    gv = _layer_norm(_gelu(za[:, A_WIDTH:]), lng, lnb)
    zb = jnp.dot(xb, win_ref[:, 2 * A_WIDTH:], preferred_element_type=F32)
    qs = _rope(zb[:, :B_WIDTH], cos, sin_signed)
    k = _rope(zb[:, B_WIDTH:B_WIDTH + KV_WIDTH], cos, sin_signed)
    v = zb[:, B_WIDTH + KV_WIDTH:]
    return u, gv, qs, k, v


def _merge(x, mix, nag, nbg, wout_ref, ln1g, ln1b):
    mi = jnp.concatenate([_rms_norm(mix[:, :A_WIDTH], nag), _rms_norm(mix[:, A_WIDTH:], nbg)], 1)
    m = jnp.dot(mi.astype(BF16), wout_ref[:, :D_MODEL], preferred_element_type=F32)
    return _post_norm(x + m, ln1g, ln1b)


def _conv_taps(a, prev8):
    r1 = pltpu.roll(a, 1, 0)
    r2 = pltpu.roll(a, 2, 0)
    row = _row_iota(prev8.shape)
    first1 = jnp.where(row < 1, pltpu.roll(prev8, 1, 0), r1[:SUBLANES])
    first2 = jnp.where(row < 2, pltpu.roll(prev8, 2, 0), r2[:SUBLANES])
    return (jnp.concatenate([first1, r1[SUBLANES:]], 0), jnp.concatenate([first2, r2[SUBLANES:]], 0))


def _prompt_kernel(sinks_ref, x_ref, cos_ref, sin_ref, win_ref, lng_ref, lnb_ref, ws_ref, bs_ref,
                   nag_ref, nbg_ref, wout_ref, ln1g_ref, ln1b_ref, wg_ref, wu_ref, cw_ref, cb_ref,
                   wd_ref, ln2g_ref, ln2b_ref,
                   y_ref, pk_ref, pv_ref, pc_ref,
                   kv_scr, conv_scr, mix_scr, h_scr, pre1_scr, x1c_scr, x1b_scr, xb_scr, mib_scr,
                   *, tile, seq_tiles, n_steps):
    s = pl.program_id(0)
    mixer_tile0 = TILES_PER_STEP * jnp.minimum(s, n_steps - 2)
    ffn_tile0 = TILES_PER_STEP * jnp.maximum(s - 1, 0)

    @pl.when(s == 0)
    def _():
        pre1_scr[...] = jnp.zeros_like(pre1_scr)

    @pl.when(mixer_tile0 % seq_tiles == 0)
    def _():
        kv_scr[...] = jnp.zeros_like(kv_scr)

    @pl.when(ffn_tile0 % seq_tiles == 0)
    def _():
        conv_scr[...] = jnp.zeros_like(conv_scr)

    row_blocks = [slice(j * ROW_BLOCK, (j + 1) * ROW_BLOCK) for j in range(tile // ROW_BLOCK)]
    attn_blocks = [(j, g) for j in range(tile // ROW_BLOCK) for g in range(B_KV_HEADS)]
    n_proj = win_ref.shape[1] // PROJ_BLOCK
    assert n_proj == 7 and len(row_blocks) == 2 and len(attn_blocks) == 4 and N_FF_CHUNKS == 11

    def pipeline_slot(k):
        tm = (mixer_tile0 + k) % seq_tiles
        trows = slice(k * tile, (k + 1) * tile)
        pre1 = pre1_scr.at[k]
        st = {}

        def proj_cast():
            xb_scr[...] = x_ref[0, trows, :].astype(BF16)

        def proj_dot(i):
            cols = slice(i * PROJ_BLOCK, (i + 1) * PROJ_BLOCK)
            st['z', i] = jnp.dot(xb_scr[...], win_ref[:, cols], preferred_element_type=F32)

        def epi_u(i):
            st['u', i] = _gelu(st.pop(('z', i)))

        def epi_gv():
            g = jnp.concatenate([_gelu(st.pop(('z', 2))), _gelu(st.pop(('z', 3)))], 1)
            st['gvb'] = _layer_norm(g, lng_ref[...], lnb_ref[...]).astype(BF16)

        def rope_tables():
            row0 = pl.multiple_of(tm * tile, tile)
            return cos_ref[pl.ds(row0, tile), :], sin_ref[pl.ds(row0, tile), :]

        def epi_q(i):
            cos, sin_signed = rope_tables()
            st['qb', i - 4] = _rope(st.pop(('z', i)), cos, sin_signed).astype(BF16)

        def epi_kv():
            cos, sin_signed = rope_tables()
            zkv = st.pop(('z', 6))
            kk = _rope(zkv[:, :KV_WIDTH], cos, sin_signed)
            vv = zkv[:, KV_WIDTH:]
            pk_ref[0] = kk[tile - WINDOW:]
            pv_ref[0] = vv[tile - WINDOW:]
            k0, k1 = _replicate_kv_heads(kk)
            v0, v1 = _replicate_kv_heads(vv)
            reps = [jnp.concatenate([kv_scr[i], a.astype(BF16)], 0) for i, a in enumerate((k0, k1, v0, v1))]
            for i in range(4):
                kv_scr[i] = reps[i][tile:]
            st['reps'] = reps

        def gmlp():
            cidx_r = _row_iota((GMLP_CHUNK, GMLP_CHUNK)) // CHUNK
            cidx_c = _lane_iota((GMLP_CHUNK, GMLP_CHUNK)) // CHUNK
            gvb = st.pop('gvb')
            zero = jnp.zeros((GMLP_CHUNK, A_HEAD_DIM), BF16)
            for hp in range(A_HEADS // 2):
                ha, hb = 2 * hp, 2 * hp + 1
                wm = jnp.concatenate([jnp.where(cidx_r >= cidx_c, ws_ref[h], 0.0) for h in (ha, hb)],
                                     1).astype(BF16)
                bias = jnp.concatenate([bs_ref[ha], bs_ref[hb]], 1)
                cols = slice(ha * A_HEAD_DIM, (hb + 1) * A_HEAD_DIM)
                u = st.pop(('u', hp))
                for c in range(tile // GMLP_CHUNK):
                    rows = slice(c * GMLP_CHUNK, (c + 1) * GMLP_CHUNK)
                    ga = gvb[rows, ha * A_HEAD_DIM:(ha + 1) * A_HEAD_DIM]
                    gb = gvb[rows, hb * A_HEAD_DIM:(hb + 1) * A_HEAD_DIM]
                    rhs = jnp.concatenate([jnp.concatenate([ga, zero], 1), jnp.concatenate([zero, gb], 1)], 0)
                    sg = jnp.dot(wm, rhs, preferred_element_type=F32) + bias
                    mix_scr[rows, cols] = u[rows, :] * sg

        def attn_prep():
            qc = _row_iota((ROW_BLOCK, KEY_SPAN)) // CHUNK
            kc = _lane_iota((ROW_BLOCK, KEY_SPAN)) // CHUNK
            in_band = (kc >= qc) & (kc <= qc + WINDOW // CHUNK)
            first_lo = jnp.where(tm > 0, 0, WINDOW // CHUNK)
            st.update(in_band=in_band, allowed_first=in_band & (kc >= first_lo),
                      qmasks=_head_lane_masks(ROW_BLOCK), vmasks=_head_lane_masks(KEY_SPAN))

        def attn_qk(i):
            j, g = attn_blocks[i]
            qg = st['qb', g][row_blocks[j], :]
            st['sc', i] = _attn_scores(qg, st['reps'][g][j * ROW_BLOCK:j * ROW_BLOCK + KEY_SPAN],
                                       st['qmasks'])

        def attn_softmax(i):
            j, g = attn_blocks[i]
            sinks = [sinks_ref[g * GQA_GROUP + h] for h in range(GQA_GROUP)]
            allowed = st['allowed_first'] if j == 0 else st['in_band']
            st['p', i] = _attn_probs(st.pop(('sc', i)), allowed, sinks)

        def attn_pv(i):
            j, g = attn_blocks[i]
            out = _attn_values(st.pop(('p', i)),
                               st['reps'][2 + g][j * ROW_BLOCK:j * ROW_BLOCK + KEY_SPAN], st['vmasks'])
            mix_scr[row_blocks[j], A_WIDTH + g * Q_GROUP_WIDTH:A_WIDTH + (g + 1) * Q_GROUP_WIDTH] = out

        def rms(j):
            rows = row_blocks[j]
            mi = jnp.concatenate([_rms_norm(mix_scr[rows, :A_WIDTH], nag_ref[...]),
                                  _rms_norm(mix_scr[rows, A_WIDTH:], nbg_ref[...])], 1)
            mib_scr[rows, :] = mi.astype(BF16)

        def wout(nb):
            cols = slice(nb * WOUT_BLOCK, (nb + 1) * WOUT_BLOCK)
            m = jnp.dot(mib_scr[...], wout_ref[:, cols], preferred_element_type=F32)
            pre1[:, cols] = x_ref[0, trows, cols] + m

        def ln1(j):
            rows = row_blocks[j]
            x1 = _post_norm(pre1[rows, :], ln1g_ref[...], ln1b_ref[...])
            x1c_scr[rows, :] = x1
            x1b_scr[rows, :] = x1.astype(BF16)

        def ffn_dots(c):
            cs = slice(c * FF_CHUNK, (c + 1) * FF_CHUNK)
            x1b = x1b_scr[...]
            st['a', c] = jnp.dot(x1b, wg_ref[:, cs], preferred_element_type=F32)
            st['up', c] = jnp.dot(x1b, wu_ref[:, cs], preferred_element_type=F32)

        def ffn_epi(c):
            cs = slice(c * FF_CHUNK, (c + 1) * FF_CHUNK)
            a, up = st.pop(('a', c)), st.pop(('up', c))
            a1, a2 = _conv_taps(a, conv_scr[:SUBLANES, cs])
            cc = a2 * cw_ref[0:1, cs] + a1 * cw_ref[1:2, cs] + a * cw_ref[2:3, cs] + cb_ref[:, cs]
            h_scr[:, cs] = (_gelu(cc) * up).astype(BF16)
            conv_scr[:SUBLANES, cs] = a[tile - SUBLANES:]

        def down(nb):
            cols = slice(nb * DOWN_BLOCK, (nb + 1) * DOWN_BLOCK)
            f = jnp.dot(h_scr[...], wd_ref[:, cols], preferred_element_type=F32)
            st['pre2', nb] = x1c_scr[:, cols] + f

        def ln2():
            pre2 = jnp.concatenate([st.pop(('pre2', nb)) for nb in range(D_MODEL // DOWN_BLOCK)], 1)
            y_ref[0, trows, :] = _post_norm(pre2, ln2g_ref[...], ln2b_ref[...])

        proj_cast()
        proj_dot(0)
        ln1(0)
        proj_dot(1)
        ln1(1)
        proj_dot(2)
        proj_dot(3)
        proj_dot(4)
        epi_u(0)
        ffn_dots(0)
        epi_u(1)
        ffn_dots(1)
        ffn_epi(0)
        proj_dot(5)
        ffn_dots(2)
        ffn_epi(1)
        epi_gv()
        proj_dot(6)
        ffn_dots(3)
        ffn_epi(2)
        epi_q(4)
        ffn_dots(4)
        ffn_epi(3)
        gmlp()
        epi_q(5)
        ffn_dots(5)
        ffn_epi(4)
        epi_kv()
        attn_prep()
        for c in range(6, N_FF_CHUNKS):
            ffn_dots(c)
            ffn_epi(c - 1)
        attn_qk(0)
        attn_qk(1)
        ffn_epi(N_FF_CHUNKS - 1)
        attn_qk(2)
        attn_qk(3)
        down(0)
        attn_softmax(0)
        attn_softmax(1)
        down(1)
        attn_pv(0)
        attn_pv(1)
        attn_softmax(2)
        attn_softmax(3)
        down(2)
        attn_pv(2)
        attn_pv(3)
        rms(0)
        down(3)
        rms(1)
        wout(0)
        ln2()
        wout(1)

    for k in range(TILES_PER_STEP):
        pipeline_slot(k)
    pc_ref[0] = conv_scr[:SUBLANES, :]


def _sample_kernel(sinks_ref, x_ref, cos_ref, sin_ref, ck_ref, cv_ref, s1_ref, s2_ref,
                   win_ref, lng_ref, lnb_ref, wt_ref, bs_ref,
                   nag_ref, nbg_ref, wout_ref, ln1g_ref, ln1b_ref, wg_ref, wu_ref, cw_ref, cb_ref,
                   wd_ref, ln2g_ref, ln2b_ref,
                   y_ref, k_ref, v_ref, gv_ref, a_ref,
                   mix_scr, h_scr, *, n_batch, s_len):
    rows_all = n_batch * s_len
    x = x_ref[...]
    xb = x.astype(BF16)
    u, gv, q, k, v = _mixer_inputs(xb, win_ref, lng_ref[...], lnb_ref[...], cos_ref[...], sin_ref[...])
    gv_ref[...] = gv
    k_ref[...] = k
    v_ref[...] = v

    ri = _row_iota((rows_all, rows_all))
    ci = _lane_iota((rows_all, rows_all))
    same_batch = (ri // s_len) == (ci // s_len)
    causal = ((ri % s_len) // CHUNK) >= ((ci % s_len) // CHUNK)
    gvb = gv.astype(BF16)
    for h in range(A_HEADS):
        wm = jnp.where(same_batch & causal, wt_ref[h], 0.0).astype(BF16)
        cols = slice(h * A_HEAD_DIM, (h + 1) * A_HEAD_DIM)
        s = jnp.dot(wm, gvb[:, cols], preferred_element_type=F32) + bs_ref[h]
        mix_scr[:, cols] = u[:, cols] * s

    qb = q.astype(BF16)
    qmasks = _head_lane_masks(s_len)
    vmasks = _head_lane_masks(KEY_SPAN)
    allowed = _lane_iota((s_len, KEY_SPAN)) < WINDOW + s_len
    pad = jnp.zeros((KEY_SPAN - WINDOW - s_len, LANES), F32)
    for b in range(n_batch):
        rows = slice(b * s_len, (b + 1) * s_len)
        k_all = jnp.concatenate([ck_ref[b], k[rows], pad], 0)
        v_all = jnp.concatenate([cv_ref[b], v[rows], pad], 0)
        kreps = [a.astype(BF16) for a in _replicate_kv_heads(k_all)]
        vreps = [a.astype(BF16) for a in _replicate_kv_heads(v_all)]
        for g in range(B_KV_HEADS):
            sinks = [sinks_ref[g * GQA_GROUP + h] for h in range(GQA_GROUP)]
            qcols = slice(g * Q_GROUP_WIDTH, (g + 1) * Q_GROUP_WIDTH)
            out = _attn_block(qb[rows, qcols], kreps[g], vreps[g], allowed, sinks, qmasks, vmasks)
            mix_scr[rows, A_WIDTH + g * Q_GROUP_WIDTH:A_WIDTH + (g + 1) * Q_GROUP_WIDTH] = out

    x1 = _merge(x, mix_scr[...], nag_ref[...], nbg_ref[...], wout_ref, ln1g_ref[...], ln1b_ref[...])
    x1b = x1.astype(BF16)

    pos = _row_iota((rows_all, FF_CHUNK)) % s_len
    for c in range(N_FF_CHUNKS):
        cs = slice(c * FF_CHUNK, (c + 1) * FF_CHUNK)
        a = jnp.dot(x1b, wg_ref[:, cs], preferred_element_type=F32)
        up = jnp.dot(x1b, wu_ref[:, cs], preferred_element_type=F32)
        a_ref[:, cs] = a
        a1 = jnp.where(pos < 1, s1_ref[:, cs], pltpu.roll(a, 1, 0))
        a2 = jnp.where(pos < 2, s2_ref[:, cs], pltpu.roll(a, 2, 0))
        cc = a2 * cw_ref[0:1, cs] + a1 * cw_ref[1:2, cs] + a * cw_ref[2:3, cs] + cb_ref[:, cs]
        h_scr[:, cs] = (_gelu(cc) * up).astype(BF16)
    f = jnp.dot(h_scr[...], wd_ref[:, :D_MODEL], preferred_element_type=F32)
    y_ref[...] = _post_norm(x1 + f, ln2g_ref[...], ln2b_ref[...])


def _rope_tables(pos):
    inv = ROPE_THETA ** (-np.arange(HALF, dtype=np.float64) / HALF)
    ang = pos.astype(np.float64)[:, None] * inv[None, :]
    cos, sin = np.cos(ang), np.sin(ang)
    reps = LANES // HEAD_DIM
    return (np.tile(np.concatenate([cos, cos], -1), (1, reps)).astype(np.float32),
            np.tile(np.concatenate([-sin, sin], -1), (1, reps)).astype(np.float32))


def _resident(shape):
    return pl.BlockSpec(shape, lambda *_: (0,) * len(shape), pipeline_mode=pl.Buffered(1))


def kernel(x_prompt, x_sample, cache_k, cache_v, state_ffn_conv, w_in, gmlp_ln_g, gmlp_ln_b,
           gmlp_w_s, gmlp_b_s, attn_sinks, norm_a_g, norm_b_g, w_out, ln1_g, ln1_b,
           w_gate, w_up, conv_w, conv_b, w_down, ln2_g, ln2_b):
    assert w_in.shape[0] == DEPTH == 1
    bp, tp, _ = x_prompt.shape
    bs, ts, _ = x_sample.shape
    tile = SEQ_TILE
    step_rows = TILES_PER_STEP * tile
    assert tp % step_rows == 0 and tile % ROW_BLOCK == 0 and tile >= WINDOW
    assert bs * ts == ROW_BLOCK and WINDOW + ts <= KEY_SPAN and ts >= CONV_WIDTH - 1

    row = lambda a: a[0].reshape(1, -1)
    pad_lanes = lambda w: jnp.pad(w, ((0, 0), (0, LANES)))
    q_cols = (jnp.arange(w_in.shape[-1]) >= 2 * A_WIDTH) & (jnp.arange(w_in.shape[-1]) < 2 * A_WIDTH + B_WIDTH)
    win_b = (w_in[0] * jnp.where(q_cols, QK_SCALE, 1.0)[None, :]).astype(BF16)
    wout_b = pad_lanes((w_out[0] * (1.0 / ALPHA)).astype(BF16))
    wg_b, wu_b = w_gate[0].astype(BF16), w_up[0].astype(BF16)
    wd_b = pad_lanes((w_down[0] * (1.0 / ALPHA)).astype(BF16))
    sinks = attn_sinks[0]
    vec_args = dict(lng=row(gmlp_ln_g), lnb=row(gmlp_ln_b), nag=row(norm_a_g), nbg=row(norm_b_g),
                    ln1g=row(ln1_g), ln1b=row(ln1_b), cb=row(conv_b), ln2g=row(ln2_g), ln2b=row(ln2_b))
    cw = conv_w[0]
    smem = pl.BlockSpec(memory_space=pltpu.SMEM)

    def weight_specs(ws_shape):
        return [_resident(win_b.shape), _resident((1, A_WIDTH)), _resident((1, A_WIDTH)),
                _resident(ws_shape), _resident(ws_shape),
                _resident((1, A_WIDTH)), _resident((1, B_WIDTH)), _resident(wout_b.shape),
                _resident((1, D_MODEL)), _resident((1, D_MODEL)),
                _resident(wg_b.shape), _resident(wu_b.shape), _resident(cw.shape), _resident((1, D_FF)),
                _resident(wd_b.shape), _resident((1, D_MODEL)), _resident((1, D_MODEL))]

    def weight_args(ws, bsb):
        return (win_b, vec_args['lng'], vec_args['lnb'], ws, bsb, vec_args['nag'], vec_args['nbg'],
                wout_b, vec_args['ln1g'], vec_args['ln1b'], wg_b, wu_b, cw, vec_args['cb'], wd_b,
                vec_args['ln2g'], vec_args['ln2b'])

    cos_p, sin_p = _rope_tables(np.arange(tp))
    bias_p = jnp.broadcast_to(gmlp_b_s[0][:, :, None], (A_HEADS, GMLP_CHUNK, A_HEAD_DIM))
    seq_tiles = tp // tile
    seq_blocks = tp // step_rows
    n_blocks = bp * seq_blocks
    n_steps = n_blocks + 1
    def stage_bt(lag):
        def bt(s):
            i = jnp.clip(s - lag, 0, n_blocks - 1)
            return i // seq_blocks, i % seq_blocks
        return bt
    mixer_bt, ffn_bt = stage_bt(0), stage_bt(1)
    y_p, pk, pv, pc = pl.pallas_call(
        functools.partial(_prompt_kernel, tile=tile, seq_tiles=seq_tiles, n_steps=n_steps),
        grid=(n_steps,),
        in_specs=[smem,
                  pl.BlockSpec((1, step_rows, D_MODEL), lambda s: (*mixer_bt(s), 0)),
                  _resident(cos_p.shape), _resident(sin_p.shape)]
                 + weight_specs((A_HEADS, GMLP_CHUNK, GMLP_CHUNK)),
        out_specs=[pl.BlockSpec((1, step_rows, D_MODEL), lambda s: (*ffn_bt(s), 0)),
                   pl.BlockSpec((1, WINDOW, KV_WIDTH), lambda s: (mixer_bt(s)[0], 0, 0)),
                   pl.BlockSpec((1, WINDOW, KV_WIDTH), lambda s: (mixer_bt(s)[0], 0, 0)),
                   pl.BlockSpec((1, SUBLANES, D_FF), lambda s: (ffn_bt(s)[0], 0, 0))],
        out_shape=[jax.ShapeDtypeStruct((bp, tp, D_MODEL), F32),
                   jax.ShapeDtypeStruct((bp, WINDOW, KV_WIDTH), F32),
                   jax.ShapeDtypeStruct((bp, WINDOW, KV_WIDTH), F32),
                   jax.ShapeDtypeStruct((bp, SUBLANES, D_FF), F32)],
        scratch_shapes=[pltpu.VMEM((4, WINDOW, LANES), BF16),
                        pltpu.VMEM((2 * SUBLANES, D_FF), F32),
                        pltpu.VMEM((tile, MIX_WIDTH), F32),
                        pltpu.VMEM((tile, D_FF), BF16),
                        pltpu.VMEM((TILES_PER_STEP, tile, D_MODEL), F32),
                        pltpu.VMEM((tile, D_MODEL), F32),
                        pltpu.VMEM((tile, D_MODEL), BF16),
                        pltpu.VMEM((tile, D_MODEL), BF16),
                        pltpu.VMEM((tile, MIX_WIDTH), BF16)],
        compiler_params=pltpu.CompilerParams(dimension_semantics=("arbitrary",),
                                             vmem_limit_bytes=V7X_VMEM_LIMIT_BYTES),
        name="prompt_layer",
    )(sinks, x_prompt, cos_p, sin_p, *weight_args(gmlp_w_s[0], bias_p))

    rows_all = bs * ts
    cos_s, sin_s = _rope_tables(PAST_LEN + np.arange(ts))
    cos_s, sin_s = np.tile(cos_s, (bs, 1)), np.tile(sin_s, (bs, 1))
    ck = cache_k[0].reshape(bs, WINDOW, KV_WIDTH)
    cv = cache_v[0].reshape(bs, WINDOW, KV_WIDTH)
    st = state_ffn_conv[0]
    pad_frames = lambda a: jnp.pad(a, ((0, 0), (0, ts - a.shape[1]), (0, 0))).reshape(rows_all, D_FF)
    s1, s2 = pad_frames(st[:, 1:]), pad_frames(st)
    w_tiled = jnp.tile(gmlp_w_s[0][:, :ts, :ts], (1, bs, bs))
    bias_s = jnp.broadcast_to(jnp.tile(gmlp_b_s[0][:, :ts], (1, bs))[:, :, None],
                              (A_HEADS, rows_all, A_HEAD_DIM))
    full = lambda shape: pl.BlockSpec(shape, lambda i: (0,) * len(shape))
    y_s, k_s, v_s, gv_s, a_s = pl.pallas_call(
        functools.partial(_sample_kernel, n_batch=bs, s_len=ts),
        grid=(1,),
        in_specs=[smem, _resident((rows_all, D_MODEL)), _resident(cos_s.shape), _resident(sin_s.shape),
                  _resident(ck.shape), _resident(cv.shape), _resident(s1.shape), _resident(s2.shape)]
                 + weight_specs((A_HEADS, rows_all, rows_all)),
        out_specs=[full((rows_all, D_MODEL)), full((rows_all, KV_WIDTH)), full((rows_all, KV_WIDTH)),
                   full((rows_all, A_WIDTH)), full((rows_all, D_FF))],
        out_shape=[jax.ShapeDtypeStruct((rows_all, D_MODEL), F32),
                   jax.ShapeDtypeStruct((rows_all, KV_WIDTH), F32),
                   jax.ShapeDtypeStruct((rows_all, KV_WIDTH), F32),
                   jax.ShapeDtypeStruct((rows_all, A_WIDTH), F32),
                   jax.ShapeDtypeStruct((rows_all, D_FF), F32)],
        scratch_shapes=[pltpu.VMEM((rows_all, MIX_WIDTH), F32),
                        pltpu.VMEM((rows_all, D_FF), BF16)],
        compiler_params=pltpu.CompilerParams(dimension_semantics=("arbitrary",),
                                             vmem_limit_bytes=V7X_VMEM_LIMIT_BYTES),
        name="sample_layer",
    )(sinks, x_sample.reshape(rows_all, D_MODEL), cos_s, sin_s, ck, cv, s1, s2,
      *weight_args(w_tiled, bias_s))

    kv5 = lambda a, n, t: a.reshape(1, n, t, B_KV_HEADS, HEAD_DIM)
    return (y_p, y_s.reshape(bs, ts, D_MODEL),
            kv5(pk, bp, WINDOW), kv5(pv, bp, WINDOW),
            pc[None, :, SUBLANES - (CONV_WIDTH - 1):],
            kv5(k_s, bs, ts), kv5(v_s, bs, ts),
            gv_s.reshape(1, bs, ts, A_HEADS, A_HEAD_DIM),
            a_s.reshape(bs, ts, D_FF)[None, :, ts - (CONV_WIDTH - 1):])
```
